```python
import math
import jax
import jax.numpy as jnp
from jax import lax
import numpy as np

D_MODEL = 2048
BATCH = 4
SEQ = 2048
DEPTH = 4
DEC_BATCH = 128
DEC_SEQ = 1
PAST_LEN = 16384
PAGE_SIZE = 128

CHUNK = 128
EPS = 1e-6
SSD_D_INNER = D_MODEL
SSD_HEAD_DIM = 64
SSD_HEADS = SSD_D_INNER // SSD_HEAD_DIM
SSD_GROUPS = 4
SSD_STATE = 128
SSD_CONV = 4
SSD_CONV_DIM = SSD_D_INNER + 2 * SSD_GROUPS * SSD_STATE
GM_D = D_MODEL
GM_GROUPS = 8
GM_GROUP_DIM = GM_D // GM_GROUPS
M_HEADS = 8
M_DV = D_MODEL // M_HEADS
M_DQK = M_DV // 2
N_BRANCH = 3
BRANCH_D = D_MODEL
D_FF = 5632
FFN_CONV = 3
IN_SPLITS = (SSD_D_INNER, SSD_CONV_DIM, SSD_HEADS, 2 * GM_D, M_HEADS * M_DQK, M_HEADS * M_DQK, M_HEADS * M_DV, M_HEADS * M_DV, M_HEADS, M_HEADS, N_BRANCH * D_MODEL)
D_IN = sum(IN_SPLITS)

kernel_name = 'hybrid_ssd_gmlp_mlstm_decoder_step'


def rmsnorm(x, g):
    xf = x.astype(jnp.float32)
    y = xf * lax.rsqrt(jnp.mean(xf * xf, axis=-1, keepdims=True) + EPS)
    return (y * g.astype(jnp.float32)).astype(x.dtype)


def group_rmsnorm(x, g, groups):
    shp = x.shape
    xg = x.reshape(shp[:-1] + (groups, shp[-1] // groups))
    return rmsnorm(xg, g.reshape(groups, shp[-1] // groups)).reshape(shp)


def causal_dwconv(x, prev, w, b):
    width = w.shape[0]
    L = x.shape[1]
    xp = jnp.concatenate([prev.astype(x.dtype), x], axis=1)
    y = sum(xp[:, j:j + L] * w[j] for j in range(width)) + b
    return y, xp[:, L:]


def _to_chunks(t, nc, q):
    return jnp.moveaxis(t.reshape((t.shape[0], nc, q) + t.shape[2:]), 1, 0)


def _from_chunks(t):
    t = jnp.moveaxis(t, 0, 1)
    return t.reshape((t.shape[0], t.shape[1] * t.shape[2]) + t.shape[3:])


def ssd_scan(x, dt, a, bm, cm, h0):
    f32 = jnp.float32
    bsz, L, H, P = x.shape
    G, N = bm.shape[2], bm.shape[3]
    hg = H // G
    qc = CHUNK if L % CHUNK == 0 else L
    nc = L // qc
    mask = jnp.tril(jnp.ones((qc, qc), bool))
    a_g = a.astype(f32).reshape(G, hg)
    xs = (_to_chunks(x.astype(f32).reshape(bsz, L, G, hg, P), nc, qc),
          _to_chunks(dt.astype(f32).reshape(bsz, L, G, hg), nc, qc),
          _to_chunks(bm.astype(f32), nc, qc),
          _to_chunks(cm.astype(f32), nc, qc))

    def step(h, inp):
        xx, dd, bb, cc = inp
        s = jnp.cumsum(dd * a_g, axis=1)
        decay = jnp.exp(jnp.where(mask[None, :, :, None, None], s[:, :, None] - s[:, None], -jnp.inf))
        w = jnp.einsum('btgn,bsgn->btsg', cc, bb)[..., None] * decay * dd[:, None]
        y = jnp.einsum('btsgh,bsghp->btghp', w, xx)
        y = y + jnp.einsum('btgn,bghpn->btghp', cc, h) * jnp.exp(s)[..., None]
        w_end = jnp.exp(s[:, -1:] - s) * dd
        h_new = h * jnp.exp(s[:, -1])[..., None, None] + jnp.einsum('bsgh,bsghp,bsgn->bghpn', w_end, xx, bb)
        return h_new, y

    h_last, ys = lax.scan(step, h0.astype(f32).reshape(bsz, G, hg, P, N), xs)
    return _from_chunks(ys).reshape(bsz, L, H, P), h_last.reshape(bsz, H, P, N)


def mlstm_scan(q, k, v, logi, logf, c0, n0, m0):
    f32 = jnp.float32
    L = q.shape[1]
    qc = CHUNK if L % CHUNK == 0 else L
    nc = L // qc
    mask = jnp.tril(jnp.ones((qc, qc), bool))
    xs = tuple(_to_chunks(t.astype(f32), nc, qc) for t in (q, k, v, logi, logf))

    def step(carry, inp):
        c, n, m = carry
        qq, kk, vv, li, lf = inp
        b = jnp.cumsum(lf, axis=1)
        log_d = jnp.where(mask[None, :, :, None], b[:, :, None] - b[:, None] + li[:, None], -jnp.inf)
        inter = b + m[:, None]
        s = jnp.maximum(inter, jnp.max(log_d, axis=2))
        w_inter = jnp.exp(inter - s)
        qk = jnp.einsum('bthd,bshd->btsh', qq, kk) * jnp.exp(log_d - s[:, :, None])
        num = jnp.einsum('btsh,bshe->bthe', qk, vv) + w_inter[..., None] * jnp.einsum('bthd,bhde->bthe', qq, c)
        den = jnp.sum(qk, axis=2) + w_inter * jnp.einsum('bthd,bhd->bth', qq, n)
        h = num / jnp.maximum(jnp.abs(den), jnp.exp(-s))[..., None]
        m_new = s[:, -1]
        wk = jnp.exp(b[:, -1:] - b + li - m_new[:, None])
        decay = jnp.exp(b[:, -1] + m - m_new)
        c_new = decay[..., None, None] * c + jnp.einsum('bsh,bshd,bshe->bhde', wk, kk, vv)
        n_new = decay[..., None] * n + jnp.einsum('bsh,bshd->bhd', wk, kk)
        return (c_new, n_new, m_new), h

    (c, n, m), hs = lax.scan(step, (c0.astype(f32), n0.astype(f32), m0.astype(f32)), xs)
    return _from_chunks(hs), c, n, m


def chunk_spatial_gate(u, v, w_s, b_s):
    bsz, L, _ = v.shape
    lc = CHUNK if L >= CHUNK else L
    lp = -(-L // lc) * lc
    nc = lp // lc
    vp = jnp.pad(v, ((0, 0), (0, lp - L), (0, 0))).reshape(bsz, nc, lc, GM_GROUPS, GM_GROUP_DIM)
    w = jnp.tril(w_s[:, :lc, :lc])
    bias = jnp.swapaxes(b_s[:, :lc], 0, 1)
    mixed = jnp.einsum('gts,bcsgd->bctgd', w, vp) + bias[:, :, None]
    return u * mixed.reshape(bsz, lp, GM_D)[:, :L]


def layer_forward(x, h_ssd, conv_ssd, c_m, n_m, m_m, conv_ffn,
                  norm1, w_in, ssd_conv_w, ssd_conv_b, ssd_dt_bias, ssd_a_log, ssd_d, ssd_norm,
                  gm_norm, gm_ws, gm_bs, m_i_bias, m_f_bias, m_norm, w_branch, w_out,
                  norm2, w_gate, w_up, ffn_conv_w, ffn_conv_b, w_down):
    f32 = jnp.float32
    bsz, L, _ = x.shape
    h = rmsnorm(x, norm1)
    proj = h @ w_in
    cuts = np.cumsum(IN_SPLITS)[:-1].tolist()
    z, xbc, dt_raw, uv, q, k, v, o, ig, fg, gl = jnp.split(proj, cuts, axis=-1)

    xbc, conv_ssd_new = causal_dwconv(xbc, conv_ssd, ssd_conv_w, ssd_conv_b)
    xbc = jax.nn.silu(xbc)
    xs, bm, cm = jnp.split(xbc, [SSD_D_INNER, SSD_D_INNER + SSD_GROUPS * SSD_STATE], axis=-1)
    xs = xs.reshape(bsz, L, SSD_HEADS, SSD_HEAD_DIM)
    dt = jax.nn.softplus(dt_raw.astype(f32) + ssd_dt_bias.astype(f32))
    a = -jnp.exp(ssd_a_log.astype(f32))
    y, h_ssd_new = ssd_scan(xs, dt, a, bm.reshape(bsz, L, SSD_GROUPS, SSD_STATE),
                            cm.reshape(bsz, L, SSD_GROUPS, SSD_STATE), h_ssd)
    y = (y + ssd_d.astype(f32)[:, None] * xs.astype(f32)).astype(x.dtype)
    y_ssd = group_rmsnorm(y.reshape(bsz, L, SSD_D_INNER) * jax.nn.silu(z), ssd_norm, SSD_GROUPS)

    u, vg = jnp.split(jax.nn.gelu(uv), 2, axis=-1)
    vg = rmsnorm(vg, gm_norm)
    y_gm = chunk_spatial_gate(u, vg, gm_ws, gm_bs)

    qh = q.reshape(bsz, L, M_HEADS, M_DQK)
    kh = k.reshape(bsz, L, M_HEADS, M_DQK) * (M_DQK ** -0.5)
    vh = v.reshape(bsz, L, M_HEADS, M_DV)
    logi = ig.astype(f32) + m_i_bias.astype(f32)
    logf = jax.nn.log_sigmoid(fg.astype(f32) + m_f_bias.astype(f32))
    hm, c_new, n_new, m_new = mlstm_scan(qh, kh, vh, logi, logf, c_m, n_m, m_m)
    hm = rmsnorm(hm.astype(x.dtype), m_norm.reshape(M_HEADS, M_DV))
    y_m = hm.reshape(bsz, L, BRANCH_D) * jax.nn.sigmoid(o)

    branches = jnp.stack([y_ssd, y_gm, y_m], axis=2)
    bproj = jnp.einsum('blnc,ncd->blnd', branches, w_branch)
    gates = jax.nn.sigmoid(gl).reshape(bsz, L, N_BRANCH, D_MODEL)
    x = x + jnp.sum(gates * bproj, axis=2) @ w_out

    h2 = rmsnorm(x, norm2)
    gpre, conv_ffn_new = causal_dwconv(h2 @ w_gate, conv_ffn, ffn_conv_w, ffn_conv_b)
    x = x + (jax.nn.silu(gpre) * (h2 @ w_up)) @ w_down
    return x, (h_ssd_new, conv_ssd_new, c_new, n_new, m_new, conv_ffn_new, vg)


def setup_inputs(seed: int = 0) -> dict:
    key = jax.random.key(seed)
    ks = iter(jax.random.split(key, 40))
    nrm = jax.random.normal

    def gain(shape):
        return 1.0 + 0.02 * nrm(next(ks), shape)

    x_prompt = nrm(next(ks), (BATCH, SEQ, D_MODEL))
    x_sample = nrm(next(ks), (DEC_BATCH, DEC_SEQ, D_MODEL))
    state_ssd = nrm(next(ks), (DEPTH, DEC_BATCH, SSD_HEADS, SSD_HEAD_DIM, SSD_STATE)) * SSD_STATE ** -0.5
    state_ssd_conv = nrm(next(ks), (DEPTH, DEC_BATCH, SSD_CONV - 1, SSD_CONV_DIM))
    state_mlstm_c = nrm(next(ks), (DEPTH, DEC_BATCH, M_HEADS, M_DQK, M_DV))
    state_mlstm_n = nrm(next(ks), (DEPTH, DEC_BATCH, M_HEADS, M_DQK))
    state_mlstm_m = 0.5 * nrm(next(ks), (DEPTH, DEC_BATCH, M_HEADS))
    state_ffn_conv = nrm(next(ks), (DEPTH, DEC_BATCH, FFN_CONV - 1, D_FF))

    norm1 = gain((DEPTH, D_MODEL))
    w_in = nrm(next(ks), (DEPTH, D_MODEL, D_IN)) * D_MODEL ** -0.5
    ssd_conv_w = nrm(next(ks), (DEPTH, SSD_CONV, SSD_CONV_DIM)) * SSD_CONV ** -0.5
    ssd_conv_b = 0.02 * nrm(next(ks), (DEPTH, SSD_CONV_DIM))
    dt0 = jnp.exp(jax.random.uniform(next(ks), (DEPTH, SSD_HEADS), minval=math.log(1e-3), maxval=math.log(1e-1)))
    ssd_dt_bias = dt0 + jnp.log(-jnp.expm1(-dt0))
    ssd_a_log = jnp.log(jax.random.uniform(next(ks), (DEPTH, SSD_HEADS), minval=1.0, maxval=16.0))
    ssd_d = gain((DEPTH, SSD_HEADS))
    ssd_norm = gain((DEPTH, SSD_D_INNER))
    gm_norm = gain((DEPTH, GM_D))
    gm_ws = nrm(next(ks), (DEPTH, GM_GROUPS, CHUNK, CHUNK)) * CHUNK ** -0.5
    gm_bs = 1.0 + 0.1 * nrm(next(ks), (DEPTH, GM_GROUPS, CHUNK))
    m_i_bias = 0.1 * nrm(next(ks), (DEPTH, M_HEADS))
    m_f_bias = jnp.linspace(3.0, 6.0, M_HEADS)[None] + 0.1 * nrm(next(ks), (DEPTH, M_HEADS))
    m_norm = gain((DEPTH, M_HEADS * M_DV))
    w_branch = nrm(next(ks), (DEPTH, N_BRANCH, BRANCH_D, D_MODEL)) * BRANCH_D ** -0.5
    w_out = nrm(next(ks), (DEPTH, D_MODEL, D_MODEL)) * D_MODEL ** -0.5
    norm2 = gain((DEPTH, D_MODEL))
    w_gate = nrm(next(ks), (DEPTH, D_MODEL, D_FF)) * D_MODEL ** -0.5
    w_up = nrm(next(ks), (DEPTH, D_MODEL, D_FF)) * D_MODEL ** -0.5
    ffn_conv_w = nrm(next(ks), (DEPTH, FFN_CONV, D_FF)) * FFN_CONV ** -0.5
    ffn_conv_b = 0.02 * nrm(next(ks), (DEPTH, D_FF))
    w_down = nrm(next(ks), (DEPTH, D_FF, D_MODEL)) * D_FF ** -0.5
    final_norm = gain((D_MODEL,))
    return {'x_prompt': x_prompt, 'x_sample': x_sample,
            'state_ssd': state_ssd, 'state_ssd_conv': state_ssd_conv,
            'state_mlstm_c': state_mlstm_c, 'state_mlstm_n': state_mlstm_n, 'state_mlstm_m': state_mlstm_m,
            'state_ffn_conv': state_ffn_conv,
            'norm1': norm1, 'w_in': w_in, 'ssd_conv_w': ssd_conv_w, 'ssd_conv_b': ssd_conv_b,
            'ssd_dt_bias': ssd_dt_bias, 'ssd_a_log': ssd_a_log, 'ssd_d': ssd_d, 'ssd_norm': ssd_norm,
            'gm_norm': gm_norm, 'gm_ws': gm_ws, 'gm_bs': gm_bs,
            'm_i_bias': m_i_bias, 'm_f_bias': m_f_bias, 'm_norm': m_norm,
            'w_branch': w_branch, 'w_out': w_out, 'norm2': norm2,
            'w_gate': w_gate, 'w_up': w_up, 'ffn_conv_w': ffn_conv_w, 'ffn_conv_b': ffn_conv_b,
            'w_down': w_down, 'final_norm': final_norm}


def reference(x_prompt, x_sample, state_ssd, state_ssd_conv, state_mlstm_c, state_mlstm_n, state_mlstm_m,
              state_ffn_conv, norm1, w_in, ssd_conv_w, ssd_conv_b, ssd_dt_bias, ssd_a_log, ssd_d, ssd_norm,
              gm_norm, gm_ws, gm_bs, m_i_bias, m_f_bias, m_norm, w_branch, w_out, norm2,
              w_gate, w_up, ffn_conv_w, ffn_conv_b, w_down, final_norm):
    f32 = jnp.float32
    bp = x_prompt.shape[0]
    xp, xq = x_prompt, x_sample
    p_new = [[] for _ in range(6)]
    s_new = [[] for _ in range(7)]
    for l in range(DEPTH):
        lw = (norm1[l], w_in[l], ssd_conv_w[l], ssd_conv_b[l], ssd_dt_bias[l], ssd_a_log[l], ssd_d[l], ssd_norm[l],
              gm_norm[l], gm_ws[l], gm_bs[l], m_i_bias[l], m_f_bias[l], m_norm[l], w_branch[l], w_out[l],
              norm2[l], w_gate[l], w_up[l], ffn_conv_w[l], ffn_conv_b[l], w_down[l])
        xp, sp = layer_forward(xp,
                               jnp.zeros((bp, SSD_HEADS, SSD_HEAD_DIM, SSD_STATE), f32),
                               jnp.zeros((bp, SSD_CONV - 1, SSD_CONV_DIM), xp.dtype),
                               jnp.zeros((bp, M_HEADS, M_DQK, M_DV), f32),
                               jnp.zeros((bp, M_HEADS, M_DQK), f32),
                               jnp.zeros((bp, M_HEADS), f32),
                               jnp.zeros((bp, FFN_CONV - 1, D_FF), xp.dtype),
                               *lw)
        xq, sq = layer_forward(xq, state_ssd[l], state_ssd_conv[l], state_mlstm_c[l], state_mlstm_n[l],
                               state_mlstm_m[l], state_ffn_conv[l], *lw)
        for i in range(6):
            p_new[i].append(sp[i])
        for i in range(7):
            s_new[i].append(sq[i])
    y_prompt = rmsnorm(xp, final_norm)
    y_sample = rmsnorm(xq, final_norm)
    p_ssd = jnp.stack(p_new[0])
    p_ssd_conv = jnp.stack(p_new[1])
    p_mlstm_c = jnp.stack(p_new[2])
    p_mlstm_n = jnp.stack(p_new[3])
    p_mlstm_m = jnp.stack(p_new[4])
    p_ffn_conv = jnp.stack(p_new[5])
    s_ssd = jnp.stack(s_new[0])
    s_ssd_conv = jnp.stack(s_new[1])
    s_mlstm_c = jnp.stack(s_new[2])
    s_mlstm_n = jnp.stack(s_new[3])
    s_mlstm_m = jnp.stack(s_new[4])
    s_ffn_conv = jnp.stack(s_new[5])
    s_gmlp_v = jnp.stack(s_new[6])
    return (y_prompt, y_sample, p_ssd, p_ssd_conv, p_mlstm_c, p_mlstm_n, p_mlstm_m, p_ffn_conv,
            s_ssd, s_ssd_conv, s_mlstm_c, s_mlstm_n, s_mlstm_m, s_ffn_conv, s_gmlp_v)
```

```python
import functools

import jax
import jax.numpy as jnp
from jax import lax
from jax.experimental import pallas as pl
from jax.experimental.pallas import tpu as pltpu

F32 = jnp.float32
BF16 = jnp.bfloat16

D_MODEL = 2048
DEPTH = 4
CHUNK = 128
EPS = 1e-6
SSD_HEADS = 32
SSD_HEAD_DIM = 64
SSD_GROUPS = 4
SSD_STATE = 128
SSD_GROUP_COLS = D_MODEL // SSD_GROUPS
GM_GROUPS = 8
GM_GROUP_DIM = 256
M_HEADS = 8
M_DV = 256
M_DQK = 128
D_FF = 5632

OFF_Z = 0
OFF_X = 2048
OFF_U = 4096
OFF_VG = 6144
OFF_V = 8192
OFF_O = 10240
OFF_G = 12288
OFF_Q = 18432
OFF_K = 19456
OFF_B = 20480
OFF_C = 20992
OFF_S = 21504
N_PACK = 21632
S_DT = 0
S_IG = 32
S_FG = 40

VMEM_LIMIT_BYTES = 56 * 1024 * 1024
SAMPLE_ROWS_PER_STEP = 8


def _cp(*sem):
    return pltpu.CompilerParams(dimension_semantics=sem, vmem_limit_bytes=VMEM_LIMIT_BYTES)


def _dot(a, b):
    return jnp.dot(a, b, preferred_element_type=F32)


def _dot_rt(a, b):
    return lax.dot_general(a, b, (((1,), (1,)), ((), ())), preferred_element_type=F32)


def _dot_lt(a, b):
    return lax.dot_general(a, b, (((0,), (0,)), ((), ())), preferred_element_type=F32)


def _split3(a):
    a1 = a.astype(BF16)
    r1 = a - a1.astype(F32)
    a2 = r1.astype(BF16)
    a3 = (r1 - a2.astype(F32)).astype(BF16)
    return a1, a2, a3


def _dot3_l(a_f32, b_bf16):
    a1, a2, a3 = _split3(a_f32)
    return (_dot(a1, b_bf16) + _dot(a2, b_bf16)) + _dot(a3, b_bf16)


def _dot3_r(a_bf16, b_f32):
    b1, b2, b3 = _split3(b_f32)
    return (_dot(a_bf16, b1) + _dot(a_bf16, b2)) + _dot(a_bf16, b3)


def _tri(n, lower=True):
    r = lax.broadcasted_iota(jnp.int32, (n, n), 0)
    c = lax.broadcasted_iota(jnp.int32, (n, n), 1)
    return (r >= c) if lower else (r <= c)


def _softplus(x):
    return jnp.maximum(x, 0.0) + jnp.log1p(jnp.exp(-jnp.abs(x)))


def _log_sigmoid(x):
    return -_softplus(-x)


def _silu(x):
    return x * jax.nn.sigmoid(x)


def _rms_kernel(x_ref, g_ref, o_ref):
    x = x_ref[...]
    y = x * lax.rsqrt(jnp.mean(x * x, axis=-1, keepdims=True) + EPS)
    o_ref[...] = (y * g_ref[...]).astype(o_ref.dtype)


def _rmsnorm_rows(x, g, out_dtype, tm, row_block0=0, n_rows=None):
    m_total, d = x.shape
    n_rows = m_total if n_rows is None else n_rows
    return pl.pallas_call(
        _rms_kernel,
        grid=(n_rows // tm,),
        in_specs=[pl.BlockSpec((tm, d), lambda i: (row_block0 + i, 0)),
                  pl.BlockSpec((1, d), lambda i: (0, 0))],
        out_specs=pl.BlockSpec((tm, d), lambda i: (i, 0)),
        out_shape=jax.ShapeDtypeStruct((n_rows, d), out_dtype),
        compiler_params=_cp("parallel"),
        name="rmsnorm_rows",
    )(x, g.reshape(1, d))


def _mm_kernel(a_ref, w_ref, o_ref):
    o_ref[...] = _dot(a_ref[...], w_ref[...]).astype(o_ref.dtype)


def _mm_res_kernel(a_ref, w_ref, r_ref, o_ref):
    o_ref[...] = r_ref[...] + _dot(a_ref[...], w_ref[...])


def _matmul(a, w, layer, tm, tn, residual=None):
    m, k = a.shape
    n = w.shape[-1]
    in_specs = [pl.BlockSpec((tm, k), lambda i, j: (i, 0)),
                pl.BlockSpec((None, k, tn), lambda i, j: (layer, 0, j))]
    args = [a, w]
    kern = _mm_kernel
    if residual is not None:
        in_specs.append(pl.BlockSpec((tm, tn), lambda i, j: (i, j)))
        args.append(residual)
        kern = _mm_res_kernel
    return pl.pallas_call(
        kern,
        grid=(m // tm, n // tn),
        in_specs=in_specs,
        out_specs=pl.BlockSpec((tm, tn), lambda i, j: (i, j)),
        out_shape=jax.ShapeDtypeStruct((m, n), F32),
        compiler_params=_cp("parallel", "arbitrary"),
        name="matmul_res" if residual is not None else "matmul",
    )(*args)


def _merge_kernel(y0_ref, y1_ref, y2_ref, w_ref, g0_ref, g1_ref, g2_ref, o_ref):
    acc = jax.nn.sigmoid(g0_ref[...]) * _dot(y0_ref[...], w_ref[0])
    acc = acc + jax.nn.sigmoid(g1_ref[...]) * _dot(y1_ref[...], w_ref[1])
    acc = acc + jax.nn.sigmoid(g2_ref[...]) * _dot(y2_ref[...], w_ref[2])
    o_ref[...] = acc.astype(o_ref.dtype)


def _merge(y0, y1, y2, w_branch, layer, proj, tm, tn):
    m = y0.shape[0]
    gb = OFF_G // tn
    gstep = D_MODEL // tn
    yspec = pl.BlockSpec((tm, D_MODEL), lambda i, j: (i, 0))

    def gspec(b):
        return pl.BlockSpec((tm, tn), lambda i, j: (i, gb + b * gstep + j))

    return pl.pallas_call(
        _merge_kernel,
        grid=(m // tm, D_MODEL // tn),
        in_specs=[yspec, yspec, yspec,
                  pl.BlockSpec((None, 3, D_MODEL, tn), lambda i, j: (layer, 0, 0, j)),
                  gspec(0), gspec(1), gspec(2)],
        out_specs=pl.BlockSpec((tm, tn), lambda i, j: (i, j)),
        out_shape=jax.ShapeDtypeStruct((m, D_MODEL), BF16),
        compiler_params=_cp("parallel", "arbitrary"),
        name="merge",
    )(y0, y1, y2, w_branch, proj, proj, proj)


def _pair_select(lane_lo, col_a, col_b):
    return jnp.where(lane_lo, col_a, col_b)


def _ssd_prompt_kernel(z_ref, x_ref, b_ref, c_ref, sm_ref,
                       cwx_ref, cbx_ref, cwb_ref, cbb_ref, cwc_ref, cbc_ref,
                       dtb_row_ref, dtb_col_ref, alog_row_ref, alog_col_ref, dfull_ref, norm_ref,
                       y_ref, hout_ref,
                       xpad, bpad, cpad, ht_scr, ybuf):
    c = pl.program_id(1)
    nc = pl.num_programs(1)
    q = CHUNK

    @pl.when(c == 0)
    def _():
        xpad[0:8, :] = jnp.zeros((8, xpad.shape[1]), F32)
        bpad[0:8, :] = jnp.zeros((8, bpad.shape[1]), F32)
        cpad[0:8, :] = jnp.zeros((8, cpad.shape[1]), F32)
        ht_scr[...] = jnp.zeros(ht_scr.shape, F32)

    @pl.when(c > 0)
    def _():
        xpad[0:8, :] = xpad[q:q + 8, :]
        bpad[0:8, :] = bpad[q:q + 8, :]
        cpad[0:8, :] = cpad[q:q + 8, :]

    xpad[8:q + 8, :] = x_ref[...]
    bpad[8:q + 8, :] = b_ref[...]
    cpad[8:q + 8, :] = c_ref[...]

    def conv(pad, w_ref, bias_ref, lanes):
        acc = pad[5:q + 5, lanes] * w_ref[0:1, lanes]
        acc = acc + pad[6:q + 6, lanes] * w_ref[1:2, lanes]
        acc = acc + pad[7:q + 7, lanes] * w_ref[2:3, lanes]
        acc = acc + pad[8:q + 8, lanes] * w_ref[3:4, lanes]
        return _silu(acc + bias_ref[:, lanes])

    bm = conv(bpad, cwb_ref, cbb_ref, slice(None))
    cm = conv(cpad, cwc_ref, cbc_ref, slice(None))

    sm = sm_ref[...]
    sm_t = sm.T
    dt_col = _softplus(sm[:, S_DT:S_DT + SSD_HEADS] + dtb_row_ref[...])
    dt_row = _softplus(sm_t[S_DT:S_DT + SSD_HEADS, :] + dtb_col_ref[...])
    da_col = dt_col * (-jnp.exp(alog_row_ref[...]))
    da_row = dt_row * (-jnp.exp(alog_col_ref[...]))
    tril = _tri(q, True)
    s_col = _dot3_r(tril.astype(BF16), da_col)
    s_row = _dot3_l(da_row, _tri(q, False).astype(BF16))
    es_col = jnp.exp(s_col)
    s_last = s_col[q - 1:q, :]
    wend_col = jnp.exp(s_last - s_col) * dt_col
    elast = jnp.exp(s_last)

    lane_lo = lax.broadcasted_iota(jnp.int32, (q, 128), 1) < SSD_HEAD_DIM
    lane_lo1 = lax.broadcasted_iota(jnp.int32, (1, 128), 1) < SSD_HEAD_DIM

    for g in range(SSD_GROUPS):
        bg = bm[:, g * SSD_STATE:(g + 1) * SSD_STATE]
        cg = cm[:, g * SSD_STATE:(g + 1) * SSD_STATE].astype(BF16)
        bg_t = bg.T.astype(BF16)
        cb = _dot(cg, bg_t)
        gsl = slice(g * SSD_GROUP_COLS, (g + 1) * SSD_GROUP_COLS)
        yint = _dot(cg, ht_scr[:, gsl].astype(BF16))
        for jj in range(4):
            j = g * 4 + jj
            ha, hb = 2 * j, 2 * j + 1
            psl = slice(j * 128, (j + 1) * 128)
            xp = conv(xpad, cwx_ref, cbx_ref, psl)
            xp_bf = xp.astype(BF16)
            ys = []
            for h in (ha, hb):
                dec = jnp.exp(jnp.where(tril, s_col[:, h:h + 1] - s_row[h:h + 1, :], -jnp.inf))
                w = cb * dec * dt_row[h:h + 1, :]
                ys.append(_dot(w.astype(BF16), xp_bf))
            y = jnp.where(lane_lo, ys[0], ys[1])
            y = y + yint[:, jj * 128:(jj + 1) * 128] * _pair_select(lane_lo, es_col[:, ha:ha + 1], es_col[:, hb:hb + 1])
            y = y + dfull_ref[:, psl] * xp
            ybuf[:, psl] = y * _silu(z_ref[:, psl])
            xw = xp * _pair_select(lane_lo, wend_col[:, ha:ha + 1], wend_col[:, hb:hb + 1])
            dpair = _pair_select(lane_lo1, elast[:, ha:ha + 1], elast[:, hb:hb + 1])
            ht_scr[:, psl] = ht_scr[:, psl] * dpair + _dot(bg_t, xw.astype(BF16))
        yg = ybuf[:, gsl]
        yn = yg * lax.rsqrt(jnp.mean(yg * yg, axis=-1, keepdims=True) + EPS)
        y_ref[:, gsl] = (yn * norm_ref[:, gsl]).astype(y_ref.dtype)

    @pl.when(c == nc - 1)
    def _():
        hout_ref[...] = ht_scr[...].T


def _ssd_prompt(proj, n_batch, n_chunks, n_rows_total, p):
    q = CHUNK
    rb = lambda b, c: b * n_chunks + c

    def pspec(width, off):
        return pl.BlockSpec((q, width), lambda b, c: (rb(b, c), off // width))

    def wspec(shape):
        return pl.BlockSpec(shape, lambda b, c: (0,) * len(shape))

    return pl.pallas_call(
        _ssd_prompt_kernel,
        grid=(n_batch, n_chunks),
        in_specs=[pspec(2048, OFF_Z), pspec(2048, OFF_X), pspec(512, OFF_B), pspec(512, OFF_C), pspec(128, OFF_S),
                  wspec((4, 2048)), wspec((1, 2048)), wspec((4, 512)), wspec((1, 512)), wspec((4, 512)), wspec((1, 512)),
                  wspec((1, 32)), wspec((32, 1)), wspec((1, 32)), wspec((32, 1)), wspec((1, 2048)), wspec((1, 2048))],
        out_specs=[pl.BlockSpec((q, D_MODEL), lambda b, c: (rb(b, c), 0)),
                   pl.BlockSpec((None, D_MODEL, SSD_STATE), lambda b, c: (b, 0, 0))],
        out_shape=[jax.ShapeDtypeStruct((n_rows_total, D_MODEL), BF16),
                   jax.ShapeDtypeStruct((n_batch, D_MODEL, SSD_STATE), F32)],
        scratch_shapes=[pltpu.VMEM((q + 8, 2048), F32), pltpu.VMEM((q + 8, 512), F32), pltpu.VMEM((q + 8, 512), F32),
                        pltpu.VMEM((SSD_STATE, D_MODEL), F32), pltpu.VMEM((q, D_MODEL), F32)],
        compiler_params=_cp("parallel", "arbitrary"),
        name="ssd_prompt",
    )(proj, proj, proj, proj, proj,
      p["cwx"], p["cbx"], p["cwb"], p["cbb"], p["cwc"], p["cbc"],
      p["dtb_row"], p["dtb_col"], p["alog_row"], p["alog_col"], p["dfull"], p["ssd_norm"])


def _ssd_sample_kernel(z_ref, x_ref, b_ref, c_ref, sm_ref, cs0_ref, cs1_ref, cs2_ref,
                       cwx_ref, cbx_ref, cwb_ref, cbb_ref, cwc_ref, cbc_ref,
                       dtb_row_ref, alog_row_ref, dfull_ref, norm_ref, hin_ref,
                       y_ref, hout_ref):
    r8 = SAMPLE_ROWS_PER_STEP

    def conv(lo, hi, new, w_ref, bias_ref):
        acc = cs0_ref[:, lo:hi] * w_ref[0:1, :]
        acc = acc + cs1_ref[:, lo:hi] * w_ref[1:2, :]
        acc = acc + cs2_ref[:, lo:hi] * w_ref[2:3, :]
        acc = acc + new * w_ref[3:4, :]
        return _silu(acc + bias_ref[...])

    xs = conv(0, 2048, x_ref[...], cwx_ref, cbx_ref)
    bm = conv(2048, 2560, b_ref[...], cwb_ref, cbb_ref)
    cm = conv(2560, 3072, c_ref[...], cwc_ref, cbc_ref)
    dt = _softplus(sm_ref[:, S_DT:S_DT + SSD_HEADS] + dtb_row_ref[...])
    e = jnp.exp(dt * (-jnp.exp(alog_row_ref[...])))
    hh = lax.broadcasted_iota(jnp.int32, (SSD_HEADS, D_MODEL), 0)
    cc = lax.broadcasted_iota(jnp.int32, (SSD_HEADS, D_MODEL), 1)
    expand = jnp.where((cc >= hh * SSD_HEAD_DIM) & (cc < (hh + 1) * SSD_HEAD_DIM), 1.0, 0.0).astype(BF16)
    dt_full = _dot3_l(dt, expand)
    e_full = _dot3_l(e, expand)
    dx = xs * dt_full
    e1, e2, e3 = (t.astype(F32) for t in _split3(e_full))
    row = lax.broadcasted_iota(jnp.int32, (r8, SSD_GROUP_COLS), 0)
    row_n = lax.broadcasted_iota(jnp.int32, (r8, SSD_STATE), 0)
    ones = jnp.ones((r8, SSD_STATE), BF16)

    ygroups = []
    for g in range(SSD_GROUPS):
        gsl = slice(g * SSD_GROUP_COLS, (g + 1) * SSD_GROUP_COLS)
        nsl = slice(g * SSD_STATE, (g + 1) * SSD_STATE)
        bg = bm[:, nsl].astype(BF16)
        cg = cm[:, nsl]
        dxg = dx[:, gsl]
        yacc = jnp.zeros((r8, SSD_GROUP_COLS), F32)
        for r in range(r8):
            a_e = jnp.where(row == 0, e1[r:r + 1, gsl],
                            jnp.where(row == 1, e2[r:r + 1, gsl],
                                      jnp.where(row == 2, e3[r:r + 1, gsl], 0.0))).astype(BF16)
            ecol = _dot_lt(a_e, ones)
            a_x = jnp.where(row == r, dxg, 0.0).astype(BF16)
            hn = hin_ref[r, gsl, :] * ecol + _dot_lt(a_x, bg)
            hout_ref[r, gsl, :] = hn
            c_r = jnp.where(row_n == r, cg, 0.0).astype(BF16)
            yacc = yacc + _dot_rt(c_r, hn.astype(BF16))
        ygroups.append(yacc)

    for g in range(SSD_GROUPS):
        gsl = slice(g * SSD_GROUP_COLS, (g + 1) * SSD_GROUP_COLS)
        y = ygroups[g] + dfull_ref[:, gsl] * xs[:, gsl]
        y = y * _silu(z_ref[:, gsl])
        yn = y * lax.rsqrt(jnp.mean(y * y, axis=-1, keepdims=True) + EPS)
        y_ref[:, gsl] = yn * norm_ref[:, gsl]


def _ssd_sample(proj, row0, n_dec, conv_state2d, h_state, p):
    r8 = SAMPLE_ROWS_PER_STEP
    rb0 = row0 // r8

    def pspec(width, off):
        return pl.BlockSpec((r8, width), lambda i: (rb0 + i, off // width))

    def wspec(shape):
        return pl.BlockSpec(shape, lambda i: (0,) * len(shape))

    def cspec(j):
        return pl.BlockSpec((r8, 3072), lambda i: (i, j))

    return pl.pallas_call(
        _ssd_sample_kernel,
        grid=(n_dec // r8,),
        in_specs=[pspec(2048, OFF_Z), pspec(2048, OFF_X), pspec(512, OFF_B), pspec(512, OFF_C), pspec(128, OFF_S),
                  cspec(0), cspec(1), cspec(2),
                  wspec((4, 2048)), wspec((1, 2048)), wspec((4, 512)), wspec((1, 512)), wspec((4, 512)), wspec((1, 512)),
                  wspec((1, 32)), wspec((1, 32)), wspec((1, 2048)), wspec((1, 2048)),
                  pl.BlockSpec((r8, D_MODEL, SSD_STATE), lambda i: (i, 0, 0))],
        out_specs=[pl.BlockSpec((r8, D_MODEL), lambda i: (i, 0)),
                   pl.BlockSpec((r8, D_MODEL, SSD_STATE), lambda i: (i, 0, 0))],
        out_shape=[jax.ShapeDtypeStruct((n_dec, D_MODEL), F32),
                   jax.ShapeDtypeStruct((n_dec, D_MODEL, SSD_STATE), F32)],
        compiler_params=_cp("parallel"),
        name="ssd_sample",
    )(proj, proj, proj, proj, proj, conv_state2d, conv_state2d, conv_state2d,
      p["cwx"], p["cbx"], p["cwb"], p["cbb"], p["cwc"], p["cbc"],
      p["dtb_row"], p["alog_row"], p["dfull"], p["ssd_norm"], h_state)


GM_ROWS_PER_STEP = 256


def _gm_prompt_kernel(u_ref, v_ref, gnorm_ref, ws_ref, bst_ref, y_ref):
    q = CHUNK
    tril = _tri(q, True)
    for cc in range(GM_ROWS_PER_STEP // q):
        rsl = slice(cc * q, (cc + 1) * q)
        u = jax.nn.gelu(u_ref[rsl, :])
        v = jax.nn.gelu(v_ref[rsl, :])
        vn = v * lax.rsqrt(jnp.mean(v * v, axis=-1, keepdims=True) + EPS) * gnorm_ref[...]
        for g in range(GM_GROUPS):
            gsl = slice(g * GM_GROUP_DIM, (g + 1) * GM_GROUP_DIM)
            w = jnp.where(tril, ws_ref[g], 0.0).astype(BF16)
            mixed = _dot(w, vn[:, gsl].astype(BF16)) + bst_ref[:, g:g + 1]
            y_ref[rsl, gsl] = (u[:, gsl] * mixed).astype(y_ref.dtype)


def _gm_prompt(proj, n_prompt_rows, n_rows_total, p):
    r = GM_ROWS_PER_STEP
    return pl.pallas_call(
        _gm_prompt_kernel,
        grid=(n_prompt_rows // r,),
        in_specs=[pl.BlockSpec((r, 2048), lambda i: (i, OFF_U // 2048)),
                  pl.BlockSpec((r, 2048), lambda i: (i, OFF_VG // 2048)),
                  pl.BlockSpec((1, 2048), lambda i: (0, 0)),
                  pl.BlockSpec((GM_GROUPS, CHUNK, CHUNK), lambda i: (0, 0, 0)),
                  pl.BlockSpec((CHUNK, GM_GROUPS), lambda i: (0, 0))],
        out_specs=pl.BlockSpec((r, D_MODEL), lambda i: (i, 0)),
        out_shape=jax.ShapeDtypeStruct((n_rows_total, D_MODEL), BF16),
        compiler_params=_cp("parallel"),
        name="gm_prompt",
    )(proj, proj, p["gm_norm"], p["gm_ws"], p["gm_bst"])


def _gm_sample_pack_kernel(u_ref, v_ref, gnorm_ref, w0_ref, b0_ref, yssd_ref, ym_ref,
                           y0_any, y1_any, y2_any,
                           o0_ref, o1_ref, o2_ref, vn_ref):
    del y0_any, y1_any, y2_any
    u = jax.nn.gelu(u_ref[...])
    v = jax.nn.gelu(v_ref[...])
    vn = v * lax.rsqrt(jnp.mean(v * v, axis=-1, keepdims=True) + EPS) * gnorm_ref[...]
    vn_ref[...] = vn
    mixed = w0_ref[...] * vn + b0_ref[...]
    o0_ref[...] = yssd_ref[...].astype(o0_ref.dtype)
    o1_ref[...] = (u * mixed).astype(o1_ref.dtype)
    o2_ref[...] = ym_ref[...].astype(o2_ref.dtype)


def _gm_sample_pack(proj, row0, n_dec, p, yssd_s, ym_s, y0, y1, y2):
    rb = row0 // n_dec
    full = lambda shape: pl.BlockSpec(shape, lambda i: (0,) * len(shape))
    anyspec = pl.BlockSpec(memory_space=pl.ANY)
    ospec = pl.BlockSpec((n_dec, D_MODEL), lambda i: (rb, 0))
    return pl.pallas_call(
        _gm_sample_pack_kernel,
        grid=(1,),
        in_specs=[pl.BlockSpec((n_dec, 2048), lambda i: (rb, OFF_U // 2048)),
                  pl.BlockSpec((n_dec, 2048), lambda i: (rb, OFF_VG // 2048)),
                  full((1, 2048)), full((1, 2048)), full((1, 2048)),
                  full((n_dec, D_MODEL)), full((n_dec, D_MODEL)),
                  anyspec, anyspec, anyspec],
        out_specs=[ospec, ospec, ospec, full((n_dec, D_MODEL))],
        out_shape=[jax.ShapeDtypeStruct(y0.shape, y0.dtype), jax.ShapeDtypeStruct(y1.shape, y1.dtype),
                   jax.ShapeDtypeStruct(y2.shape, y2.dtype), jax.ShapeDtypeStruct((n_dec, D_MODEL), F32)],
        input_output_aliases={7: 0, 8: 1, 9: 2},
        compiler_params=_cp("arbitrary"),
        name="gm_sample_pack",
    )(proj, proj, p["gm_norm"], p["gm_w0"], p["gm_b0"], yssd_s, ym_s, y0, y1, y2)


def _ml_prompt_kernel(q_ref, k_ref, v_ref, o_ref, sm_ref,
                      ib_row_ref, ib_col_ref, fb_row_ref, fb_col_ref, mnorm_ref,
                      y_ref, c_ref, n_ref, m_ref):
    c = pl.program_id(1)
    q = CHUNK

    @pl.when(c == 0)
    def _():
        c_ref[...] = jnp.zeros(c_ref.shape, F32)
        n_ref[...] = jnp.zeros(n_ref.shape, F32)
        m_ref[...] = jnp.zeros(m_ref.shape, F32)

    sm = sm_ref[...]
    sm_t = sm.T
    li_col = sm[:, S_IG:S_IG + M_HEADS] + ib_row_ref[...]
    lf_col = _log_sigmoid(sm[:, S_FG:S_FG + M_HEADS] + fb_row_ref[...])
    li_row = sm_t[S_IG:S_IG + M_HEADS, :] + ib_col_ref[...]
    lf_row = _log_sigmoid(sm_t[S_FG:S_FG + M_HEADS, :] + fb_col_ref[...])
    tril = _tri(q, True)
    b_col = _dot3_r(tril.astype(BF16), lf_col)
    b_row = _dot3_l(lf_row, _tri(q, False).astype(BF16))
    b_last = b_col[q - 1:q, :]

    for h in range(M_HEADS):
        ksl = slice(h * M_DQK, (h + 1) * M_DQK)
        vsl = slice(h * M_DV, (h + 1) * M_DV)
        m_prev = m_ref[h:h + 1, 0:1]
        bc = b_col[:, h:h + 1]
        log_d = jnp.where(tril, bc - b_row[h:h + 1, :] + li_row[h:h + 1, :], -jnp.inf)
        inter = bc + m_prev
        s = jnp.maximum(inter, jnp.max(log_d, axis=1, keepdims=True))
        w_inter = jnp.exp(inter - s)
        qh = q_ref[:, ksl]
        qh_bf = qh.astype(BF16)
        kh = k_ref[:, ksl] * (M_DQK ** -0.5)
        vh_bf = v_ref[:, vsl].astype(BF16)
        qk = _dot_rt(qh_bf, kh.astype(BF16)) * jnp.exp(log_d - s)
        c_prev = c_ref[h]
        n_prev = n_ref[h:h + 1, :]
        num = _dot(qk.astype(BF16), vh_bf) + w_inter * _dot(qh_bf, c_prev.astype(BF16))
        den = jnp.sum(qk, axis=1, keepdims=True) + w_inter * jnp.sum(qh * n_prev, axis=1, keepdims=True)
        hm = num / jnp.maximum(jnp.abs(den), jnp.exp(-s))
        hn = hm * lax.rsqrt(jnp.mean(hm * hm, axis=-1, keepdims=True) + EPS) * mnorm_ref[:, vsl]
        y_ref[:, vsl] = (hn * jax.nn.sigmoid(o_ref[:, vsl])).astype(y_ref.dtype)

        m_new = s[q - 1:q, :]
        wk = jnp.exp(b_last[:, h:h + 1] - bc + li_col[:, h:h + 1] - m_new)
        decay = jnp.exp(b_last[:, h:h + 1] + m_prev - m_new)
        kw = kh * wk
        c_ref[h] = decay * c_prev + _dot_lt(kw.astype(BF16), vh_bf)
        n_ref[h:h + 1, :] = decay * n_prev + jnp.sum(kw, axis=0, keepdims=True)
        m_ref[h:h + 1, :] = jnp.broadcast_to(m_new, (1, m_ref.shape[1]))


def _ml_prompt(proj, n_batch, n_chunks, n_rows_total, p):
    q = CHUNK
    rb = lambda b, c: b * n_chunks + c

    def pspec(width, off):
        return pl.BlockSpec((q, width), lambda b, c: (rb(b, c), off // width))

    def wspec(shape):
        return pl.BlockSpec(shape, lambda b, c: (0,) * len(shape))

    return pl.pallas_call(
        _ml_prompt_kernel,
        grid=(n_batch, n_chunks),
        in_specs=[pspec(1024, OFF_Q), pspec(1024, OFF_K), pspec(2048, OFF_V), pspec(2048, OFF_O), pspec(128, OFF_S),
                  wspec((1, 8)), wspec((8, 1)), wspec((1, 8)), wspec((8, 1)), wspec((1, 2048))],
        out_specs=[pl.BlockSpec((q, D_MODEL), lambda b, c: (rb(b, c), 0)),
                   pl.BlockSpec((None, M_HEADS, M_DQK, M_DV), lambda b, c: (b, 0, 0, 0)),
                   pl.BlockSpec((None, M_HEADS, M_DQK), lambda b, c: (b, 0, 0)),
                   pl.BlockSpec((None, M_HEADS, 128), lambda b, c: (b, 0, 0))],
        out_shape=[jax.ShapeDtypeStruct((n_rows_total, D_MODEL), BF16),
                   jax.ShapeDtypeStruct((n_batch, M_HEADS, M_DQK, M_DV), F32),
                   jax.ShapeDtypeStruct((n_batch, M_HEADS, M_DQK), F32),
                   jax.ShapeDtypeStruct((n_batch, M_HEADS, 128), F32)],
        compiler_params=_cp("parallel", "arbitrary"),
        name="ml_prompt",
    )(proj, proj, proj, proj, proj,
      p["ib_row"], p["ib_col"], p["fb_row"], p["fb_col"], p["m_norm"])


def _ml_sample_kernel(q_ref, k_ref, v_ref, o_ref, sm_ref, ib_row_ref, fb_row_ref, mnorm_ref,
                      cin_ref, nin_ref, min_ref,
                      y_ref, cout_ref, nout_ref, mout_ref):
    r8 = SAMPLE_ROWS_PER_STEP
    li = sm_ref[:, S_IG:S_IG + M_HEADS] + ib_row_ref[...]
    lf = _log_sigmoid(sm_ref[:, S_FG:S_FG + M_HEADS] + fb_row_ref[...])
    m_prev = min_ref[...]
    inter = lf + m_prev
    s = jnp.maximum(inter, li)
    w_inter = jnp.exp(inter - s)
    w_in = jnp.exp(li - s)
    mout_ref[...] = s
    row_k = lax.broadcasted_iota(jnp.int32, (r8, M_DQK), 0)

    for h in range(M_HEADS):
        ksl = slice(h * M_DQK, (h + 1) * M_DQK)
        vsl = slice(h * M_DV, (h + 1) * M_DV)
        qh = q_ref[:, ksl]
        kh = k_ref[:, ksl] * (M_DQK ** -0.5)
        vh = v_ref[:, vsl]
        vh_bf = vh.astype(BF16)
        n_prev = nin_ref[:, ksl]
        wi = w_inter[:, h:h + 1]
        qk = jnp.sum(qh * kh, axis=1, keepdims=True) * w_in[:, h:h + 1]
        kw = kh * w_in[:, h:h + 1]
        qc = jnp.zeros((r8, M_DV), F32)
        for r in range(r8):
            c_prev = cin_ref[r, h]
            q_r = jnp.where(row_k == r, qh, 0.0).astype(BF16)
            qc = qc + _dot(q_r, c_prev.astype(BF16))
            kw_r = jnp.where(row_k == r, kw, 0.0).astype(BF16)
            cout_ref[r, h] = w_inter[r:r + 1, h:h + 1] * c_prev + _dot_lt(kw_r, vh_bf)
        nout_ref[:, ksl] = wi * n_prev + kw
        num = qk * vh + wi * qc
        den = qk + wi * jnp.sum(qh * n_prev, axis=1, keepdims=True)
        hm = num / jnp.maximum(jnp.abs(den), jnp.exp(-s[:, h:h + 1]))
        hn = hm * lax.rsqrt(jnp.mean(hm * hm, axis=-1, keepdims=True) + EPS) * mnorm_ref[:, vsl]
        y_ref[:, vsl] = hn * jax.nn.sigmoid(o_ref[:, vsl])


def _ml_sample(proj, row0, n_dec, c_state, n_state2d, m_state, p):
    r8 = SAMPLE_ROWS_PER_STEP
    rb0 = row0 // r8

    def pspec(width, off):
        return pl.BlockSpec((r8, width), lambda i: (rb0 + i, off // width))

    def wspec(shape):
        return pl.BlockSpec(shape, lambda i: (0,) * len(shape))

    cspec = pl.BlockSpec((r8, M_HEADS, M_DQK, M_DV), lambda i: (i, 0, 0, 0))
    nspec = pl.BlockSpec((r8, M_HEADS * M_DQK), lambda i: (i, 0))
    mspec = pl.BlockSpec((r8, M_HEADS), lambda i: (i, 0))
    return pl.pallas_call(
        _ml_sample_kernel,
        grid=(n_dec // r8,),
        in_specs=[pspec(1024, OFF_Q), pspec(1024, OFF_K), pspec(2048, OFF_V), pspec(2048, OFF_O), pspec(128, OFF_S),
                  wspec((1, 8)), wspec((1, 8)), wspec((1, 2048)), cspec, nspec, mspec],
        out_specs=[pl.BlockSpec((r8, D_MODEL), lambda i: (i, 0)), cspec, nspec, mspec],
        out_shape=[jax.ShapeDtypeStruct((n_dec, D_MODEL), F32),
                   jax.ShapeDtypeStruct(c_state.shape, F32),
                   jax.ShapeDtypeStruct(n_state2d.shape, F32),
                   jax.ShapeDtypeStruct(m_state.shape, F32)],
        compiler_params=_cp("parallel"),
        name="ml_sample",
    )(proj, proj, proj, proj, proj, p["ib_row"], p["fb_row"], p["m_norm"], c_state, n_state2d, m_state)


FFN_ROWS = 512
FFN_COLS = 1408


def _ffn_act_prompt_kernel(g_ref, u_ref, w_ref, b_ref, o_ref, gpad):
    rc = pl.program_id(2)
    r = FFN_ROWS

    @pl.when(rc == 0)
    def _():
        gpad[0:8, :] = jnp.zeros((8, gpad.shape[1]), F32)

    @pl.when(rc > 0)
    def _():
        gpad[0:8, :] = gpad[r:r + 8, :]

    gpad[8:r + 8, :] = g_ref[...]
    acc = gpad[6:r + 6, :] * w_ref[0:1, :]
    acc = acc + gpad[7:r + 7, :] * w_ref[1:2, :]
    acc = acc + gpad[8:r + 8, :] * w_ref[2:3, :]
    o_ref[...] = (_silu(acc + b_ref[...]) * u_ref[...]).astype(o_ref.dtype)


def _ffn_act_prompt(gu, n_batch, seq, n_rows_total, p):
    r, cw = FFN_ROWS, FFN_COLS
    nrc = seq // r
    nct = D_FF // cw
    return pl.pallas_call(
        _ffn_act_prompt_kernel,
        grid=(n_batch, nct, nrc),
        in_specs=[pl.BlockSpec((r, cw), lambda b, ct, rc: (b * nrc + rc, ct)),
                  pl.BlockSpec((r, cw), lambda b, ct, rc: (b * nrc + rc, nct + ct)),
                  pl.BlockSpec((3, cw), lambda b, ct, rc: (0, ct)),
                  pl.BlockSpec((1, cw), lambda b, ct, rc: (0, ct))],
        out_specs=pl.BlockSpec((r, cw), lambda b, ct, rc: (b * nrc + rc, ct)),
        out_shape=jax.ShapeDtypeStruct((n_rows_total, D_FF), BF16),
        scratch_shapes=[pltpu.VMEM((r + 8, cw), F32)],
        compiler_params=_cp("parallel", "parallel", "arbitrary"),
        name="ffn_act_prompt",
    )(gu, gu, p["ffn_cw"], p["ffn_cb"])


def _ffn_act_sample_kernel(g_ref, u_ref, s0_ref, s1_ref, w_ref, b_ref, act_any, o_ref):
    del act_any
    acc = s0_ref[...] * w_ref[0:1, :]
    acc = acc + s1_ref[...] * w_ref[1:2, :]
    acc = acc + g_ref[...] * w_ref[2:3, :]
    o_ref[...] = (_silu(acc + b_ref[...]) * u_ref[...]).astype(o_ref.dtype)


def _ffn_act_sample(gu, row0, n_dec, conv_state2d, p, act):
    rb = row0 // n_dec
    full = lambda shape: pl.BlockSpec(shape, lambda i: (0,) * len(shape))
    return pl.pallas_call(
        _ffn_act_sample_kernel,
        grid=(1,),
        in_specs=[pl.BlockSpec((n_dec, D_FF), lambda i: (rb, 0)),
                  pl.BlockSpec((n_dec, D_FF), lambda i: (rb, 1)),
                  pl.BlockSpec((n_dec, D_FF), lambda i: (0, 0)),
                  pl.BlockSpec((n_dec, D_FF), lambda i: (0, 1)),
                  full((3, D_FF)), full((1, D_FF)),
                  pl.BlockSpec(memory_space=pl.ANY)],
        out_specs=pl.BlockSpec((n_dec, D_FF), lambda i: (rb, 0)),
        out_shape=jax.ShapeDtypeStruct(act.shape, act.dtype),
        input_output_aliases={6: 0},
        compiler_params=_cp("arbitrary"),
        name="ffn_act_sample",
    )(gu, gu, conv_state2d, conv_state2d, p["ffn_cw"], p["ffn_cb"], act)


def _pick_tile(m, candidates):
    for t in candidates:
        if m % t == 0:
            return t
    raise ValueError(f"no row tile for {m}")


def _pack_w_in(w_in):
    cols = [w_in[..., 0:2048], w_in[..., 2048:4096], w_in[..., 5152:7200], w_in[..., 7200:9248],
            w_in[..., 11296:13344], w_in[..., 13344:15392], w_in[..., 15408:21552],
            w_in[..., 9248:10272], w_in[..., 10272:11296], w_in[..., 4096:4608], w_in[..., 4608:5120],
            w_in[..., 5120:5152], w_in[..., 15392:15408],
            jnp.zeros(w_in.shape[:-1] + (N_PACK - OFF_S - 48,), w_in.dtype)]
    return jnp.concatenate(cols, axis=-1).astype(BF16)


def _layer_params(l, ssd_conv_w, ssd_conv_b, ssd_dt_bias, ssd_a_log, ssd_d, ssd_norm, gm_norm, gm_ws, gm_bs,
                  m_i_bias, m_f_bias, m_norm, ffn_conv_w, ffn_conv_b):
    cw, cb = ssd_conv_w[l], ssd_conv_b[l]
    return dict(
        cwx=cw[:, 0:2048], cbx=cb[None, 0:2048], cwb=cw[:, 2048:2560], cbb=cb[None, 2048:2560],
        cwc=cw[:, 2560:3072], cbc=cb[None, 2560:3072],
        dtb_row=ssd_dt_bias[l][None, :], dtb_col=ssd_dt_bias[l][:, None],
        alog_row=ssd_a_log[l][None, :], alog_col=ssd_a_log[l][:, None],
        dfull=jnp.repeat(ssd_d[l], SSD_HEAD_DIM)[None, :], ssd_norm=ssd_norm[l][None, :],
        gm_norm=gm_norm[l][None, :], gm_ws=gm_ws[l], gm_bst=gm_bs[l].T,
        gm_w0=jnp.repeat(gm_ws[l][:, 0, 0], GM_GROUP_DIM)[None, :],
        gm_b0=jnp.repeat(gm_bs[l][:, 0], GM_GROUP_DIM)[None, :],
        ib_row=m_i_bias[l][None, :], ib_col=m_i_bias[l][:, None],
        fb_row=m_f_bias[l][None, :], fb_col=m_f_bias[l][:, None], m_norm=m_norm[l][None, :],
        ffn_cw=ffn_conv_w[l], ffn_cb=ffn_conv_b[l][None, :],
    )


def kernel(x_prompt, x_sample, state_ssd, state_ssd_conv, state_mlstm_c, state_mlstm_n, state_mlstm_m,
           state_ffn_conv, norm1, w_in, ssd_conv_w, ssd_conv_b, ssd_dt_bias, ssd_a_log, ssd_d, ssd_norm,
           gm_norm, gm_ws, gm_bs, m_i_bias, m_f_bias, m_norm, w_branch, w_out, norm2,
           w_gate, w_up, ffn_conv_w, ffn_conv_b, w_down, final_norm):
    n_batch, seq, d = x_prompt.shape
    n_dec = x_sample.shape[0]
    depth = w_in.shape[0]
    n_chunks = seq // CHUNK
    n_prompt = n_batch * seq
    n_rows = n_prompt + n_dec
    tm = _pick_tile(n_rows, (832, 640, 384, 128))

    w_pack = _pack_w_in(w_in)
    w_branch_bf = w_branch.astype(BF16)
    w_out_bf = w_out.astype(BF16)
    w_gu_bf = jnp.concatenate([w_gate, w_up], axis=-1).astype(BF16)
    w_down_bf = w_down.astype(BF16)

    x = jnp.concatenate([x_prompt.reshape(n_prompt, d), x_sample.reshape(n_dec, d)], axis=0)
    outs = [[] for _ in range(13)]
    for l in range(depth):
        p = _layer_params(l, ssd_conv_w, ssd_conv_b, ssd_dt_bias, ssd_a_log, ssd_d, ssd_norm, gm_norm, gm_ws, gm_bs,
                          m_i_bias, m_f_bias, m_norm, ffn_conv_w, ffn_conv_b)
        h = _rmsnorm_rows(x, norm1[l], BF16, tm)
        proj = _matmul(h, w_pack, l, tm, 1664)

        y0, p_ssd = _ssd_prompt(proj, n_batch, n_chunks, n_rows, p)
        y1 = _gm_prompt(proj, n_prompt, n_rows, p)
        y2, p_c, p_n, p_m = _ml_prompt(proj, n_batch, n_chunks, n_rows, p)

        ys0, s_ssd = _ssd_sample(proj, n_prompt, n_dec, state_ssd_conv[l].reshape(n_dec, -1),
                                 state_ssd[l].reshape(n_dec, D_MODEL, SSD_STATE), p)
        ys2, s_c, s_n, s_m = _ml_sample(proj, n_prompt, n_dec, state_mlstm_c[l],
                                        state_mlstm_n[l].reshape(n_dec, -1), state_mlstm_m[l], p)
        y0, y1, y2, s_vn = _gm_sample_pack(proj, n_prompt, n_dec, p, ys0, ys2, y0, y1, y2)

        merged = _merge(y0, y1, y2, w_branch_bf, l, proj, tm, 256)
        x = _matmul(merged, w_out_bf, l, tm, 512, residual=x)

        h2 = _rmsnorm_rows(x, norm2[l], BF16, tm)
        gu = _matmul(h2, w_gu_bf, l, tm, 512)
        act = _ffn_act_prompt(gu, n_batch, seq, n_rows, p)
        act = _ffn_act_sample(gu, n_prompt, n_dec, state_ffn_conv[l].reshape(n_dec, -1), p, act)
        x = _matmul(act, w_down_bf, l, tm, 512, residual=x)

        def xbc(rows):
            return jnp.concatenate([rows[..., OFF_X:OFF_X + 2048], rows[..., OFF_B:OFF_B + 512],
                                    rows[..., OFF_C:OFF_C + 512]], axis=-1)

        proj_p = proj[:n_prompt].reshape(n_batch, seq, N_PACK)
        gu_p = gu[:n_prompt].reshape(n_batch, seq, 2 * D_FF)
        outs[0].append(p_ssd.reshape(n_batch, SSD_HEADS, SSD_HEAD_DIM, SSD_STATE))
        outs[1].append(xbc(proj_p[:, seq - 3:, :]))
        outs[2].append(p_c)
        outs[3].append(p_n)
        outs[4].append(p_m[:, :, 0])
        outs[5].append(gu_p[:, seq - 2:, :D_FF])
        outs[6].append(s_ssd.reshape(n_dec, SSD_HEADS, SSD_HEAD_DIM, SSD_STATE))
        outs[7].append(jnp.concatenate([state_ssd_conv[l][:, 1:], xbc(proj[n_prompt:])[:, None, :]], axis=1))
        outs[8].append(s_c)
        outs[9].append(s_n.reshape(n_dec, M_HEADS, M_DQK))
        outs[10].append(s_m)
        outs[11].append(jnp.concatenate([state_ffn_conv[l][:, 1:], gu[n_prompt:, None, :D_FF]], axis=1))
        outs[12].append(s_vn.reshape(n_dec, 1, D_MODEL))

    tf = _pick_tile(n_prompt, (1024, 512, 256, 128))
    y_prompt = _rmsnorm_rows(x, final_norm, F32, tf, 0, n_prompt).reshape(n_batch, seq, d)
    y_sample = _rmsnorm_rows(x, final_norm, F32, n_dec, n_prompt // n_dec, n_dec).reshape(n_dec, 1, d)
    return (y_prompt, y_sample) + tuple(jnp.stack(o) for o in outs)
```

```python
import functools

import jax
import jax.numpy as jnp
from jax import lax
from jax.experimental import pallas as pl
from jax.experimental.pallas import tpu as pltpu

F32 = jnp.float32
BF16 = jnp.bfloat16

D_MODEL = 2048
DEPTH = 4
CHUNK = 128
EPS = 1e-6
SSD_HEADS = 32
SSD_HEAD_DIM = 64
SSD_GROUPS = 4
SSD_STATE = 128
SSD_GROUP_COLS = D_MODEL // SSD_GROUPS
GM_GROUPS = 8
GM_GROUP_DIM = 256
M_HEADS = 8
M_DV = 256
M_DQK = 128
D_FF = 5632

PROJ_SRC = {"z": ("a", 0), "x": ("a", 2048), "B": ("a", 4096), "C": ("a", 4608),
            "u": ("b", 0), "vg": ("b", 2048), "q": ("b", 4096), "k": ("b", 5120), "v": ("b", 6144), "o": ("b", 8192),
            "g": ("g", 0), "s": ("s", 0)}
W_IN_A = (0, 5120, 0)
W_IN_B = (5120, 10240, 32)
W_IN_G = (15360, 6144, 48)
W_IN_S_BLOCKS = (40, 120)
SM_W = 256
S_DT = 0
S_IG = 160
S_FG = 168
LANE = 128

VMEM_LIMIT_BYTES = 56 * 1024 * 1024
SAMPLE_ROWS_PER_STEP = 8


def _cp(*sem):
    return pltpu.CompilerParams(dimension_semantics=sem, vmem_limit_bytes=VMEM_LIMIT_BYTES)


def _dot(a, b):
    return jnp.dot(a, b, preferred_element_type=F32)


def _dot_rt(a, b):
    return lax.dot_general(a, b, (((1,), (1,)), ((), ())), preferred_element_type=F32)


def _dot_lt(a, b):
    return lax.dot_general(a, b, (((0,), (0,)), ((), ())), preferred_element_type=F32)


def _split3(a):
    a1 = a.astype(BF16)
    r1 = a - a1.astype(F32)
    a2 = r1.astype(BF16)
    a3 = (r1 - a2.astype(F32)).astype(BF16)
    return a1, a2, a3


def _dot3_l(a_f32, b_bf16):
    a1, a2, a3 = _split3(a_f32)
    return (_dot(a1, b_bf16) + _dot(a2, b_bf16)) + _dot(a3, b_bf16)


def _dot3_r(a_bf16, b_f32):
    b1, b2, b3 = _split3(b_f32)
    return (_dot(a_bf16, b1) + _dot(a_bf16, b2)) + _dot(a_bf16, b3)


def _tri(n, lower=True):
    r = lax.broadcasted_iota(jnp.int32, (n, n), 0)
    c = lax.broadcasted_iota(jnp.int32, (n, n), 1)
    return (r >= c) if lower else (r <= c)


def _softplus(x):
    return jnp.maximum(x, 0.0) + jnp.log1p(jnp.exp(-jnp.abs(x)))


def _log_sigmoid(x):
    return -_softplus(-x)


def _silu(x):
    return x * jax.nn.sigmoid(x)


def _rms_kernel(x_ref, g_ref, o_ref):
    x = x_ref[...]
    y = x * lax.rsqrt(jnp.mean(x * x, axis=-1, keepdims=True) + EPS)
    o_ref[...] = (y * g_ref[...]).astype(o_ref.dtype)


def _rmsnorm_rows(x, g, out_dtype, tm, row_block0=0, n_rows=None):
    m_total, d = x.shape
    n_rows = m_total if n_rows is None else n_rows
    return pl.pallas_call(
        _rms_kernel,
        grid=(n_rows // tm,),
        in_specs=[pl.BlockSpec((tm, d), lambda i: (row_block0 + i, 0)),
                  pl.BlockSpec((1, d), lambda i: (0, 0))],
        out_specs=pl.BlockSpec((tm, d), lambda i: (i, 0)),
        out_shape=jax.ShapeDtypeStruct((n_rows, d), out_dtype),
        compiler_params=_cp("parallel"),
        name="rmsnorm_rows",
    )(x, g.reshape(1, d))


def _mm_kernel(shift, has_next, has_res, *refs):
    a_ref, w_ref = refs[0], refs[1]
    pos = 2
    w = w_ref[...]
    if has_next:
        w = jnp.concatenate([w, refs[pos][...]], axis=1)[:, shift:shift + w_ref.shape[1]]
        pos += 1
    acc = _dot(a_ref[...], w.astype(BF16))
    if has_res:
        acc = refs[pos][...] + acc
        pos += 1
    o_ref = refs[pos]
    o_ref[...] = acc.astype(o_ref.dtype)


def _matmul(a, w, layer, tm, tn, src=None, residual=None):
    m, k = a.shape
    c0, n, shift = (0, w.shape[-1], 0) if src is None else src
    cb0 = c0 // tn
    in_specs = [pl.BlockSpec((tm, k), lambda i, j: (i, 0)),
                pl.BlockSpec((None, k, tn), lambda i, j: (layer, 0, cb0 + j))]
    args = [a, w]
    if shift:
        lanes_per_tile = tn // LANE
        in_specs.append(pl.BlockSpec((None, k, LANE), lambda i, j: (layer, 0, (cb0 + j + 1) * lanes_per_tile)))
        args.append(w)
    if residual is not None:
        in_specs.append(pl.BlockSpec((tm, tn), lambda i, j: (i, j)))
        args.append(residual)
    return pl.pallas_call(
        functools.partial(_mm_kernel, shift, bool(shift), residual is not None),
        grid=(m // tm, n // tn),
        in_specs=in_specs,
        out_specs=pl.BlockSpec((tm, tn), lambda i, j: (i, j)),
        out_shape=jax.ShapeDtypeStruct((m, n), F32),
        compiler_params=_cp("parallel", "arbitrary"),
        name="matmul_res" if residual is not None else "matmul",
    )(*args)


def _mm_small_kernel(a_ref, w0_ref, w1_ref, o_ref):
    a = a_ref[...]
    o_ref[:, 0:LANE] = _dot(a, w0_ref[...].astype(BF16))
    o_ref[:, LANE:2 * LANE] = _dot(a, w1_ref[...].astype(BF16))


def _matmul_small(a, w, layer, tm):
    m, k = a.shape
    b0, b1 = W_IN_S_BLOCKS
    return pl.pallas_call(
        _mm_small_kernel,
        grid=(m // tm,),
        in_specs=[pl.BlockSpec((tm, k), lambda i: (i, 0)),
                  pl.BlockSpec((None, k, LANE), lambda i: (layer, 0, b0)),
                  pl.BlockSpec((None, k, LANE), lambda i: (layer, 0, b1))],
        out_specs=pl.BlockSpec((tm, SM_W), lambda i: (i, 0)),
        out_shape=jax.ShapeDtypeStruct((m, SM_W), F32),
        compiler_params=_cp("parallel"),
        name="matmul_small",
    )(a, w, w)


def _merge_kernel(y0_ref, y1_ref, y2_ref, w_ref, g0_ref, g1_ref, g2_ref, o_ref):
    acc = jax.nn.sigmoid(g0_ref[...]) * _dot(y0_ref[...], w_ref[0].astype(BF16))
    acc = acc + jax.nn.sigmoid(g1_ref[...]) * _dot(y1_ref[...], w_ref[1].astype(BF16))
    acc = acc + jax.nn.sigmoid(g2_ref[...]) * _dot(y2_ref[...], w_ref[2].astype(BF16))
    o_ref[...] = acc.astype(o_ref.dtype)


def _merge(y0, y1, y2, w_branch, layer, proj, tm, tn):
    m = y0.shape[0]
    gb = 0
    gstep = D_MODEL // tn
    yspec = pl.BlockSpec((tm, D_MODEL), lambda i, j: (i, 0))

    def gspec(b):
        return pl.BlockSpec((tm, tn), lambda i, j: (i, gb + b * gstep + j))

    return pl.pallas_call(
        _merge_kernel,
        grid=(m // tm, D_MODEL // tn),
        in_specs=[yspec, yspec, yspec,
                  pl.BlockSpec((None, 3, D_MODEL, tn), lambda i, j: (layer, 0, 0, j)),
                  gspec(0), gspec(1), gspec(2)],
        out_specs=pl.BlockSpec((tm, tn), lambda i, j: (i, j)),
        out_shape=jax.ShapeDtypeStruct((m, D_MODEL), BF16),
        compiler_params=_cp("parallel", "arbitrary"),
        name="merge",
    )(y0, y1, y2, w_branch, proj["g"], proj["g"], proj["g"])


def _pair_select(lane_lo, col_a, col_b):
    return jnp.where(lane_lo, col_a, col_b)


def _ssd_prompt_kernel(z_ref, x_ref, b_ref, c_ref, sm_ref,
                       cwx_ref, cbx_ref, cwb_ref, cbb_ref, cwc_ref, cbc_ref,
                       dtb_row_ref, dtb_col_ref, alog_row_ref, alog_col_ref, dfull_ref, norm_ref,
                       y_ref, hout_ref,
                       xpad, bpad, cpad, ht_scr, ybuf):
    c = pl.program_id(1)
    nc = pl.num_programs(1)
    q = CHUNK

    @pl.when(c == 0)
    def _():
        xpad[0:8, :] = jnp.zeros((8, xpad.shape[1]), F32)
        bpad[0:8, :] = jnp.zeros((8, bpad.shape[1]), F32)
        cpad[0:8, :] = jnp.zeros((8, cpad.shape[1]), F32)
        ht_scr[...] = jnp.zeros(ht_scr.shape, F32)

    @pl.when(c > 0)
    def _():
        xpad[0:8, :] = xpad[q:q + 8, :]
        bpad[0:8, :] = bpad[q:q + 8, :]
        cpad[0:8, :] = cpad[q:q + 8, :]

    xpad[8:q + 8, :] = x_ref[...]
    bpad[8:q + 8, :] = b_ref[...]
    cpad[8:q + 8, :] = c_ref[...]

    def conv(pad, w_ref, bias_ref, lanes):
        acc = pad[5:q + 5, lanes] * w_ref[0:1, lanes]
        acc = acc + pad[6:q + 6, lanes] * w_ref[1:2, lanes]
        acc = acc + pad[7:q + 7, lanes] * w_ref[2:3, lanes]
        acc = acc + pad[8:q + 8, lanes] * w_ref[3:4, lanes]
        return _silu(acc + bias_ref[:, lanes])

    bm = conv(bpad, cwb_ref, cbb_ref, slice(None))
    cm = conv(cpad, cwc_ref, cbc_ref, slice(None))

    sm = sm_ref[...]
    sm_t = sm.T
    dt_col = _softplus(sm[:, S_DT:S_DT + SSD_HEADS] + dtb_row_ref[...])
    dt_row = _softplus(sm_t[S_DT:S_DT + SSD_HEADS, :] + dtb_col_ref[...])
    da_col = dt_col * (-jnp.exp(alog_row_ref[...]))
    da_row = dt_row * (-jnp.exp(alog_col_ref[...]))
    tril = _tri(q, True)
    s_col = _dot3_r(tril.astype(BF16), da_col)
    s_row = _dot3_l(da_row, _tri(q, False).astype(BF16))
    es_col = jnp.exp(s_col)
    s_last = s_col[q - 1:q, :]
    wend_col = jnp.exp(s_last - s_col) * dt_col
    elast = jnp.exp(s_last)

    lane_lo = lax.broadcasted_iota(jnp.int32, (q, 128), 1) < SSD_HEAD_DIM
    lane_lo1 = lax.broadcasted_iota(jnp.int32, (1, 128), 1) < SSD_HEAD_DIM

    for g in range(SSD_GROUPS):
        bg = bm[:, g * SSD_STATE:(g + 1) * SSD_STATE]
        cg = cm[:, g * SSD_STATE:(g + 1) * SSD_STATE].astype(BF16)
        bg_t = bg.T.astype(BF16)
        cb = _dot(cg, bg_t)
        gsl = slice(g * SSD_GROUP_COLS, (g + 1) * SSD_GROUP_COLS)
        yint = _dot(cg, ht_scr[:, gsl].astype(BF16))
        for jj in range(4):
            j = g * 4 + jj
            ha, hb = 2 * j, 2 * j + 1
            psl = slice(j * 128, (j + 1) * 128)
            xp = conv(xpad, cwx_ref, cbx_ref, psl)
            xp_bf = xp.astype(BF16)
            ys = []
            for h in (ha, hb):
                dec = jnp.exp(jnp.where(tril, s_col[:, h:h + 1] - s_row[h:h + 1, :], -jnp.inf))
                w = cb * dec * dt_row[h:h + 1, :]
                ys.append(_dot(w.astype(BF16), xp_bf))
            y = jnp.where(lane_lo, ys[0], ys[1])
            y = y + yint[:, jj * 128:(jj + 1) * 128] * _pair_select(lane_lo, es_col[:, ha:ha + 1], es_col[:, hb:hb + 1])
            y = y + dfull_ref[:, psl] * xp
            ybuf[:, psl] = y * _silu(z_ref[:, psl])
            xw = xp * _pair_select(lane_lo, wend_col[:, ha:ha + 1], wend_col[:, hb:hb + 1])
            dpair = _pair_select(lane_lo1, elast[:, ha:ha + 1], elast[:, hb:hb + 1])
            ht_scr[:, psl] = ht_scr[:, psl] * dpair + _dot(bg_t, xw.astype(BF16))
        yg = ybuf[:, gsl]
        yn = yg * lax.rsqrt(jnp.mean(yg * yg, axis=-1, keepdims=True) + EPS)
        y_ref[:, gsl] = (yn * norm_ref[:, gsl]).astype(y_ref.dtype)

    @pl.when(c == nc - 1)
    def _():
        hout_ref[...] = ht_scr[...].T


def _proj_arrays(proj, names):
    return [proj[PROJ_SRC[n][0]] for n in names]


def _ssd_prompt(proj, n_batch, n_chunks, n_rows_total, p):
    q = CHUNK
    rb = lambda b, c: b * n_chunks + c

    def pspec(width, name):
        return pl.BlockSpec((q, width), lambda b, c: (rb(b, c), PROJ_SRC[name][1] // width))

    def wspec(shape):
        return pl.BlockSpec(shape, lambda b, c: (0,) * len(shape))

    return pl.pallas_call(
        _ssd_prompt_kernel,
        grid=(n_batch, n_chunks),
        in_specs=[pspec(2048, "z"), pspec(2048, "x"), pspec(512, "B"), pspec(512, "C"), pspec(SM_W, "s"),
                  wspec((4, 2048)), wspec((1, 2048)), wspec((4, 512)), wspec((1, 512)), wspec((4, 512)), wspec((1, 512)),
                  wspec((1, 32)), wspec((32, 1)), wspec((1, 32)), wspec((32, 1)), wspec((1, 2048)), wspec((1, 2048))],
        out_specs=[pl.BlockSpec((q, D_MODEL), lambda b, c: (rb(b, c), 0)),
                   pl.BlockSpec((None, D_MODEL, SSD_STATE), lambda b, c: (b, 0, 0))],
        out_shape=[jax.ShapeDtypeStruct((n_rows_total, D_MODEL), BF16),
                   jax.ShapeDtypeStruct((n_batch, D_MODEL, SSD_STATE), F32)],
        scratch_shapes=[pltpu.VMEM((q + 8, 2048), F32), pltpu.VMEM((q + 8, 512), F32), pltpu.VMEM((q + 8, 512), F32),
                        pltpu.VMEM((SSD_STATE, D_MODEL), F32), pltpu.VMEM((q, D_MODEL), F32)],
        compiler_params=_cp("parallel", "arbitrary"),
        name="ssd_prompt",
    )(*_proj_arrays(proj, "zxBCs"),
      p["cwx"], p["cbx"], p["cwb"], p["cbb"], p["cwc"], p["cbc"],
      p["dtb_row"], p["dtb_col"], p["alog_row"], p["alog_col"], p["dfull"], p["ssd_norm"])


def _ssd_sample_kernel(z_ref, x_ref, b_ref, c_ref, sm_ref, cs0_ref, cs1_ref, cs2_ref,
                       cwx_ref, cbx_ref, cwb_ref, cbb_ref, cwc_ref, cbc_ref,
                       dtb_row_ref, alog_row_ref, dfull_ref, norm_ref, hin_ref,
                       y_ref, hout_ref):
    r8 = SAMPLE_ROWS_PER_STEP

    def conv(lo, hi, new, w_ref, bias_ref):
        acc = cs0_ref[:, lo:hi] * w_ref[0:1, :]
        acc = acc + cs1_ref[:, lo:hi] * w_ref[1:2, :]
        acc = acc + cs2_ref[:, lo:hi] * w_ref[2:3, :]
        acc = acc + new * w_ref[3:4, :]
        return _silu(acc + bias_ref[...])

    xs = conv(0, 2048, x_ref[...], cwx_ref, cbx_ref)
    bm = conv(2048, 2560, b_ref[...], cwb_ref, cbb_ref)
    cm = conv(2560, 3072, c_ref[...], cwc_ref, cbc_ref)
    dt = _softplus(sm_ref[:, S_DT:S_DT + SSD_HEADS] + dtb_row_ref[...])
    e = jnp.exp(dt * (-jnp.exp(alog_row_ref[...])))
    hh = lax.broadcasted_iota(jnp.int32, (SSD_HEADS, D_MODEL), 0)
    cc = lax.broadcasted_iota(jnp.int32, (SSD_HEADS, D_MODEL), 1)
    expand = jnp.where((cc >= hh * SSD_HEAD_DIM) & (cc < (hh + 1) * SSD_HEAD_DIM), 1.0, 0.0).astype(BF16)
    dt_full = _dot3_l(dt, expand)
    e_full = _dot3_l(e, expand)
    dx = xs * dt_full
    e1, e2, e3 = (t.astype(F32) for t in _split3(e_full))
    row = lax.broadcasted_iota(jnp.int32, (r8, SSD_GROUP_COLS), 0)
    row_n = lax.broadcasted_iota(jnp.int32, (r8, SSD_STATE), 0)
    ones = jnp.ones((r8, SSD_STATE), BF16)

    ygroups = []
    for g in range(SSD_GROUPS):
        gsl = slice(g * SSD_GROUP_COLS, (g + 1) * SSD_GROUP_COLS)
        nsl = slice(g * SSD_STATE, (g + 1) * SSD_STATE)
        bg = bm[:, nsl].astype(BF16)
        cg = cm[:, nsl]
        dxg = dx[:, gsl]
        yacc = jnp.zeros((r8, SSD_GROUP_COLS), F32)
        for r in range(r8):
            a_e = jnp.where(row == 0, e1[r:r + 1, gsl],
                            jnp.where(row == 1, e2[r:r + 1, gsl],
                                      jnp.where(row == 2, e3[r:r + 1, gsl], 0.0))).astype(BF16)
            ecol = _dot_lt(a_e, ones)
            a_x = jnp.where(row == r, dxg, 0.0).astype(BF16)
            hn = hin_ref[r, gsl, :] * ecol + _dot_lt(a_x, bg)
            hout_ref[r, gsl, :] = hn
            c_r = jnp.where(row_n == r, cg, 0.0).astype(BF16)
            yacc = yacc + _dot_rt(c_r, hn.astype(BF16))
        ygroups.append(yacc)

    for g in range(SSD_GROUPS):
        gsl = slice(g * SSD_GROUP_COLS, (g + 1) * SSD_GROUP_COLS)
        y = ygroups[g] + dfull_ref[:, gsl] * xs[:, gsl]
        y = y * _silu(z_ref[:, gsl])
        yn = y * lax.rsqrt(jnp.mean(y * y, axis=-1, keepdims=True) + EPS)
        y_ref[:, gsl] = yn * norm_ref[:, gsl]


def _drop_refs(kernel_fn, n_inputs, n_dropped):
    def body(*refs):
        return kernel_fn(*refs[:n_inputs], *refs[n_inputs + n_dropped:])
    return body


def _ssd_sample(proj, row0, n_dec, layer, conv_state_all, h_state_all, p, h_out_prev):
    r8 = SAMPLE_ROWS_PER_STEP
    rb0 = row0 // r8
    depth = h_state_all.shape[0]

    def pspec(width, name):
        return pl.BlockSpec((r8, width), lambda i: (rb0 + i, PROJ_SRC[name][1] // width))

    def wspec(shape):
        return pl.BlockSpec(shape, lambda i: (0,) * len(shape))

    def cspec(j):
        return pl.BlockSpec((None, r8, 3072), lambda i: (layer, i, j))

    hspec = pl.BlockSpec((None, r8, D_MODEL, SSD_STATE), lambda i: (layer, i, 0, 0))
    in_specs = [pspec(2048, "z"), pspec(2048, "x"), pspec(512, "B"), pspec(512, "C"), pspec(SM_W, "s"),
                cspec(0), cspec(1), cspec(2),
                wspec((4, 2048)), wspec((1, 2048)), wspec((4, 512)), wspec((1, 512)), wspec((4, 512)), wspec((1, 512)),
                wspec((1, 32)), wspec((1, 32)), wspec((1, 2048)), wspec((1, 2048)), hspec]
    args = [*_proj_arrays(proj, "zxBCs"), conv_state_all, conv_state_all, conv_state_all,
            p["cwx"], p["cbx"], p["cwb"], p["cbb"], p["cwc"], p["cbc"],
            p["dtb_row"], p["alog_row"], p["dfull"], p["ssd_norm"], h_state_all]
    n_in = len(args)
    aliases = {}
    if h_out_prev is not None:
        in_specs.append(pl.BlockSpec(memory_space=pl.ANY))
        args.append(h_out_prev)
        aliases = {n_in: 1}
    return pl.pallas_call(
        _drop_refs(_ssd_sample_kernel, n_in, len(aliases)),
        grid=(n_dec // r8,),
        in_specs=in_specs,
        out_specs=[pl.BlockSpec((r8, D_MODEL), lambda i: (i, 0)), hspec],
        out_shape=[jax.ShapeDtypeStruct((n_dec, D_MODEL), F32),
                   jax.ShapeDtypeStruct((depth, n_dec, D_MODEL, SSD_STATE), F32)],
        input_output_aliases=aliases,
        compiler_params=_cp("parallel"),
        name="ssd_sample",
    )(*args)


GM_ROWS_PER_STEP = 256


def _gm_prompt_kernel(u_ref, v_ref, gnorm_ref, ws_ref, bst_ref, y_ref):
    q = CHUNK
    tril = _tri(q, True)
    for cc in range(GM_ROWS_PER_STEP // q):
        rsl = slice(cc * q, (cc + 1) * q)
        u = jax.nn.gelu(u_ref[rsl, :])
        v = jax.nn.gelu(v_ref[rsl, :])
        vn = v * lax.rsqrt(jnp.mean(v * v, axis=-1, keepdims=True) + EPS) * gnorm_ref[...]
        for g in range(GM_GROUPS):
            gsl = slice(g * GM_GROUP_DIM, (g + 1) * GM_GROUP_DIM)
            w = jnp.where(tril, ws_ref[g], 0.0).astype(BF16)
            mixed = _dot(w, vn[:, gsl].astype(BF16)) + bst_ref[:, g:g + 1]
            y_ref[rsl, gsl] = (u[:, gsl] * mixed).astype(y_ref.dtype)


def _gm_prompt(proj, n_prompt_rows, n_rows_total, p):
    r = GM_ROWS_PER_STEP
    return pl.pallas_call(
        _gm_prompt_kernel,
        grid=(n_prompt_rows // r,),
        in_specs=[pl.BlockSpec((r, 2048), lambda i: (i, PROJ_SRC["u"][1] // 2048)),
                  pl.BlockSpec((r, 2048), lambda i: (i, PROJ_SRC["vg"][1] // 2048)),
                  pl.BlockSpec((1, 2048), lambda i: (0, 0)),
                  pl.BlockSpec((GM_GROUPS, CHUNK, CHUNK), lambda i: (0, 0, 0)),
                  pl.BlockSpec((CHUNK, GM_GROUPS), lambda i: (0, 0))],
        out_specs=pl.BlockSpec((r, D_MODEL), lambda i: (i, 0)),
        out_shape=jax.ShapeDtypeStruct((n_rows_total, D_MODEL), BF16),
        compiler_params=_cp("parallel"),
        name="gm_prompt",
    )(proj["b"], proj["b"], p["gm_norm"], p["gm_ws"], p["gm_bst"])


def _gm_sample_pack_kernel(u_ref, v_ref, gnorm_ref, w0_ref, b0_ref, yssd_ref, ym_ref,
                           y0_any, y1_any, y2_any,
                           o0_ref, o1_ref, o2_ref, vn_ref):
    del y0_any, y1_any, y2_any
    u = jax.nn.gelu(u_ref[...])
    v = jax.nn.gelu(v_ref[...])
    vn = v * lax.rsqrt(jnp.mean(v * v, axis=-1, keepdims=True) + EPS) * gnorm_ref[...]
    vn_ref[...] = vn
    mixed = w0_ref[...] * vn + b0_ref[...]
    o0_ref[...] = yssd_ref[...].astype(o0_ref.dtype)
    o1_ref[...] = (u * mixed).astype(o1_ref.dtype)
    o2_ref[...] = ym_ref[...].astype(o2_ref.dtype)


def _gm_sample_pack(proj, row0, n_dec, p, yssd_s, ym_s, y0, y1, y2):
    rb = row0 // n_dec
    full = lambda shape: pl.BlockSpec(shape, lambda i: (0,) * len(shape))
    anyspec = pl.BlockSpec(memory_space=pl.ANY)
    ospec = pl.BlockSpec((n_dec, D_MODEL), lambda i: (rb, 0))
    return pl.pallas_call(
        _gm_sample_pack_kernel,
        grid=(1,),
        in_specs=[pl.BlockSpec((n_dec, 2048), lambda i: (rb, PROJ_SRC["u"][1] // 2048)),
                  pl.BlockSpec((n_dec, 2048), lambda i: (rb, PROJ_SRC["vg"][1] // 2048)),
                  full((1, 2048)), full((1, 2048)), full((1, 2048)),
                  full((n_dec, D_MODEL)), full((n_dec, D_MODEL)),
                  anyspec, anyspec, anyspec],
        out_specs=[ospec, ospec, ospec, full((n_dec, D_MODEL))],
        out_shape=[jax.ShapeDtypeStruct(y0.shape, y0.dtype), jax.ShapeDtypeStruct(y1.shape, y1.dtype),
                   jax.ShapeDtypeStruct(y2.shape, y2.dtype), jax.ShapeDtypeStruct((n_dec, D_MODEL), F32)],
        input_output_aliases={7: 0, 8: 1, 9: 2},
        compiler_params=_cp("arbitrary"),
        name="gm_sample_pack",
    )(proj["b"], proj["b"], p["gm_norm"], p["gm_w0"], p["gm_b0"], yssd_s, ym_s, y0, y1, y2)


def _ml_prompt_kernel(q_ref, k_ref, v_ref, o_ref, sm_ref,
                      ib_row_ref, ib_col_ref, fb_row_ref, fb_col_ref, mnorm_ref,
                      y_ref, c_ref, n_ref, m_ref):
    c = pl.program_id(1)
    q = CHUNK

    @pl.when(c == 0)
    def _():
        c_ref[...] = jnp.zeros(c_ref.shape, F32)
        n_ref[...] = jnp.zeros(n_ref.shape, F32)
        m_ref[...] = jnp.zeros(m_ref.shape, F32)

    sm = sm_ref[...]
    sm_t = sm.T
    li_col = sm[:, S_IG:S_IG + M_HEADS] + ib_row_ref[...]
    lf_col = _log_sigmoid(sm[:, S_FG:S_FG + M_HEADS] + fb_row_ref[...])
    li_row = sm_t[S_IG:S_IG + M_HEADS, :] + ib_col_ref[...]
    lf_row = _log_sigmoid(sm_t[S_FG:S_FG + M_HEADS, :] + fb_col_ref[...])
    tril = _tri(q, True)
    b_col = _dot3_r(tril.astype(BF16), lf_col)
    b_row = _dot3_l(lf_row, _tri(q, False).astype(BF16))
    b_last = b_col[q - 1:q, :]

    for h in range(M_HEADS):
        ksl = slice(h * M_DQK, (h + 1) * M_DQK)
        vsl = slice(h * M_DV, (h + 1) * M_DV)
        m_prev = m_ref[h:h + 1, 0:1]
        bc = b_col[:, h:h + 1]
        log_d = jnp.where(tril, bc - b_row[h:h + 1, :] + li_row[h:h + 1, :], -jnp.inf)
        inter = bc + m_prev
        s = jnp.maximum(inter, jnp.max(log_d, axis=1, keepdims=True))
        w_inter = jnp.exp(inter - s)
        qh = q_ref[:, ksl]
        qh_bf = qh.astype(BF16)
        kh = k_ref[:, ksl] * (M_DQK ** -0.5)
        vh_bf = v_ref[:, vsl].astype(BF16)
        qk = _dot_rt(qh_bf, kh.astype(BF16)) * jnp.exp(log_d - s)
        c_prev = c_ref[h]
        n_prev = n_ref[h:h + 1, :]
        num = _dot(qk.astype(BF16), vh_bf) + w_inter * _dot(qh_bf, c_prev.astype(BF16))
        den = jnp.sum(qk, axis=1, keepdims=True) + w_inter * jnp.sum(qh * n_prev, axis=1, keepdims=True)
        hm = num / jnp.maximum(jnp.abs(den), jnp.exp(-s))
        hn = hm * lax.rsqrt(jnp.mean(hm * hm, axis=-1, keepdims=True) + EPS) * mnorm_ref[:, vsl]
        y_ref[:, vsl] = (hn * jax.nn.sigmoid(o_ref[:, vsl])).astype(y_ref.dtype)

        m_new = s[q - 1:q, :]
        wk = jnp.exp(b_last[:, h:h + 1] - bc + li_col[:, h:h + 1] - m_new)
        decay = jnp.exp(b_last[:, h:h + 1] + m_prev - m_new)
        kw = kh * wk
        c_ref[h] = decay * c_prev + _dot_lt(kw.astype(BF16), vh_bf)
        n_ref[h:h + 1, :] = decay * n_prev + jnp.sum(kw, axis=0, keepdims=True)
        m_ref[h:h + 1, :] = jnp.broadcast_to(m_new, (1, m_ref.shape[1]))


def _ml_prompt(proj, n_batch, n_chunks, n_rows_total, p):
    q = CHUNK
    rb = lambda b, c: b * n_chunks + c

    def pspec(width, name):
        return pl.BlockSpec((q, width), lambda b, c: (rb(b, c), PROJ_SRC[name][1] // width))

    def wspec(shape):
        return pl.BlockSpec(shape, lambda b, c: (0,) * len(shape))

    return pl.pallas_call(
        _ml_prompt_kernel,
        grid=(n_batch, n_chunks),
        in_specs=[pspec(1024, "q"), pspec(1024, "k"), pspec(2048, "v"), pspec(2048, "o"), pspec(SM_W, "s"),
                  wspec((1, 8)), wspec((8, 1)), wspec((1, 8)), wspec((8, 1)), wspec((1, 2048))],
        out_specs=[pl.BlockSpec((q, D_MODEL), lambda b, c: (rb(b, c), 0)),
                   pl.BlockSpec((None, M_HEADS, M_DQK, M_DV), lambda b, c: (b, 0, 0, 0)),
                   pl.BlockSpec((None, M_HEADS, M_DQK), lambda b, c: (b, 0, 0)),
                   pl.BlockSpec((None, M_HEADS, 128), lambda b, c: (b, 0, 0))],
        out_shape=[jax.ShapeDtypeStruct((n_rows_total, D_MODEL), BF16),
                   jax.ShapeDtypeStruct((n_batch, M_HEADS, M_DQK, M_DV), F32),
                   jax.ShapeDtypeStruct((n_batch, M_HEADS, M_DQK), F32),
                   jax.ShapeDtypeStruct((n_batch, M_HEADS, 128), F32)],
        compiler_params=_cp("parallel", "arbitrary"),
        name="ml_prompt",
    )(*_proj_arrays(proj, "qkvos"),
      p["ib_row"], p["ib_col"], p["fb_row"], p["fb_col"], p["m_norm"])


def _ml_sample_kernel(q_ref, k_ref, v_ref, o_ref, sm_ref, ib_row_ref, fb_row_ref, mnorm_ref,
                      cin_ref, nin_ref, min_ref,
                      y_ref, cout_ref, nout_ref, mout_ref):
    r8 = SAMPLE_ROWS_PER_STEP
    li = sm_ref[:, S_IG:S_IG + M_HEADS] + ib_row_ref[...]
    lf = _log_sigmoid(sm_ref[:, S_FG:S_FG + M_HEADS] + fb_row_ref[...])
    m_prev = min_ref[...]
    inter = lf + m_prev
    s = jnp.maximum(inter, li)
    w_inter = jnp.exp(inter - s)
    w_in = jnp.exp(li - s)
    mout_ref[...] = s
    row_k = lax.broadcasted_iota(jnp.int32, (r8, M_DQK), 0)

    for h in range(M_HEADS):
        ksl = slice(h * M_DQK, (h + 1) * M_DQK)
        vsl = slice(h * M_DV, (h + 1) * M_DV)
        qh = q_ref[:, ksl]
        kh = k_ref[:, ksl] * (M_DQK ** -0.5)
        vh = v_ref[:, vsl]
        vh_bf = vh.astype(BF16)
        n_prev = nin_ref[:, ksl]
        wi = w_inter[:, h:h + 1]
        qk = jnp.sum(qh * kh, axis=1, keepdims=True) * w_in[:, h:h + 1]
        kw = kh * w_in[:, h:h + 1]
        qc = jnp.zeros((r8, M_DV), F32)
        for r in range(r8):
            c_prev = cin_ref[r, h]
            q_r = jnp.where(row_k == r, qh, 0.0).astype(BF16)
            qc = qc + _dot(q_r, c_prev.astype(BF16))
            kw_r = jnp.where(row_k == r, kw, 0.0).astype(BF16)
            cout_ref[r, h] = w_inter[r:r + 1, h:h + 1] * c_prev + _dot_lt(kw_r, vh_bf)
        nout_ref[:, ksl] = wi * n_prev + kw
        num = qk * vh + wi * qc
        den = qk + wi * jnp.sum(qh * n_prev, axis=1, keepdims=True)
        hm = num / jnp.maximum(jnp.abs(den), jnp.exp(-s[:, h:h + 1]))
        hn = hm * lax.rsqrt(jnp.mean(hm * hm, axis=-1, keepdims=True) + EPS) * mnorm_ref[:, vsl]
        y_ref[:, vsl] = hn * jax.nn.sigmoid(o_ref[:, vsl])


def _ml_sample(proj, row0, n_dec, layer, c_all, n_all, m_all, p, prev):
    r8 = SAMPLE_ROWS_PER_STEP
    rb0 = row0 // r8

    def pspec(width, name):
        return pl.BlockSpec((r8, width), lambda i: (rb0 + i, PROJ_SRC[name][1] // width))

    def wspec(shape):
        return pl.BlockSpec(shape, lambda i: (0,) * len(shape))

    cspec = pl.BlockSpec((None, r8, M_HEADS, M_DQK, M_DV), lambda i: (layer, i, 0, 0, 0))
    nspec = pl.BlockSpec((None, r8, M_HEADS * M_DQK), lambda i: (layer, i, 0))
    mspec = pl.BlockSpec((None, r8, M_HEADS), lambda i: (layer, i, 0))
    in_specs = [pspec(1024, "q"), pspec(1024, "k"), pspec(2048, "v"), pspec(2048, "o"), pspec(SM_W, "s"),
                wspec((1, 8)), wspec((1, 8)), wspec((1, 2048)), cspec, nspec, mspec]
    args = [*_proj_arrays(proj, "qkvos"), p["ib_row"], p["fb_row"], p["m_norm"], c_all, n_all, m_all]
    n_in = len(args)
    aliases = {}
    if prev is not None:
        in_specs += [pl.BlockSpec(memory_space=pl.ANY)] * 3
        args += list(prev)
        aliases = {n_in: 1, n_in + 1: 2, n_in + 2: 3}
    return pl.pallas_call(
        _drop_refs(_ml_sample_kernel, n_in, len(aliases)),
        grid=(n_dec // r8,),
        in_specs=in_specs,
        out_specs=[pl.BlockSpec((r8, D_MODEL), lambda i: (i, 0)), cspec, nspec, mspec],
        out_shape=[jax.ShapeDtypeStruct((n_dec, D_MODEL), F32),
                   jax.ShapeDtypeStruct(c_all.shape, F32),
                   jax.ShapeDtypeStruct(n_all.shape, F32),
                   jax.ShapeDtypeStruct(m_all.shape, F32)],
        input_output_aliases=aliases,
        compiler_params=_cp("parallel"),
        name="ml_sample",
    )(*args)


FFN_ROWS = 512
FFN_COLS = 1408


def _ffn_act_prompt_kernel(g_ref, u_ref, w_ref, b_ref, o_ref, gpad):
    rc = pl.program_id(2)
    r = FFN_ROWS

    @pl.when(rc == 0)
    def _():
        gpad[0:8, :] = jnp.zeros((8, gpad.shape[1]), F32)

    @pl.when(rc > 0)
    def _():
        gpad[0:8, :] = gpad[r:r + 8, :]

    gpad[8:r + 8, :] = g_ref[...]
    acc = gpad[6:r + 6, :] * w_ref[0:1, :]
    acc = acc + gpad[7:r + 7, :] * w_ref[1:2, :]
    acc = acc + gpad[8:r + 8, :] * w_ref[2:3, :]
    o_ref[...] = (_silu(acc + b_ref[...]) * u_ref[...]).astype(o_ref.dtype)


def _ffn_act_prompt(g, u, n_batch, seq, n_rows_total, p):
    r, cw = FFN_ROWS, FFN_COLS
    nrc = seq // r
    nct = D_FF // cw
    return pl.pallas_call(
        _ffn_act_prompt_kernel,
        grid=(n_batch, nct, nrc),
        in_specs=[pl.BlockSpec((r, cw), lambda b, ct, rc: (b * nrc + rc, ct)),
                  pl.BlockSpec((r, cw), lambda b, ct, rc: (b * nrc + rc, ct)),
                  pl.BlockSpec((3, cw), lambda b, ct, rc: (0, ct)),
                  pl.BlockSpec((1, cw), lambda b, ct, rc: (0, ct))],
        out_specs=pl.BlockSpec((r, cw), lambda b, ct, rc: (b * nrc + rc, ct)),
        out_shape=jax.ShapeDtypeStruct((n_rows_total, D_FF), BF16),
        scratch_shapes=[pltpu.VMEM((r + 8, cw), F32)],
        compiler_params=_cp("parallel", "parallel", "arbitrary"),
        name="ffn_act_prompt",
    )(g, u, p["ffn_cw"], p["ffn_cb"])


def _ffn_act_sample_kernel(g_ref, u_ref, s0_ref, s1_ref, w_ref, b_ref, act_any, o_ref):
    del act_any
    acc = s0_ref[...] * w_ref[0:1, :]
    acc = acc + s1_ref[...] * w_ref[1:2, :]
    acc = acc + g_ref[...] * w_ref[2:3, :]
    o_ref[...] = (_silu(acc + b_ref[...]) * u_ref[...]).astype(o_ref.dtype)


def _ffn_act_sample(g, u, row0, n_dec, layer, conv_state_all, p, act):
    rb = row0 // n_dec
    full = lambda shape: pl.BlockSpec(shape, lambda i: (0,) * len(shape))
    return pl.pallas_call(
        _ffn_act_sample_kernel,
        grid=(1,),
        in_specs=[pl.BlockSpec((n_dec, D_FF), lambda i: (rb, 0)),
                  pl.BlockSpec((n_dec, D_FF), lambda i: (rb, 0)),
                  pl.BlockSpec((None, n_dec, D_FF), lambda i: (layer, 0, 0)),
                  pl.BlockSpec((None, n_dec, D_FF), lambda i: (layer, 0, 1)),
                  full((3, D_FF)), full((1, D_FF)),
                  pl.BlockSpec(memory_space=pl.ANY)],
        out_specs=pl.BlockSpec((n_dec, D_FF), lambda i: (rb, 0)),
        out_shape=jax.ShapeDtypeStruct(act.shape, act.dtype),
        input_output_aliases={6: 0},
        compiler_params=_cp("arbitrary"),
        name="ffn_act_sample",
    )(g, u, conv_state_all, conv_state_all, p["ffn_cw"], p["ffn_cb"], act)


def _pick_tile(m, candidates):
    for t in candidates:
        if m % t == 0:
            return t
    raise ValueError(f"no row tile for {m}")


def _layer_params(l, ssd_conv_w, ssd_conv_b, ssd_dt_bias, ssd_a_log, ssd_d, ssd_norm, gm_norm, gm_ws, gm_bs,
                  m_i_bias, m_f_bias, m_norm, ffn_conv_w, ffn_conv_b):
    cw, cb = ssd_conv_w[l], ssd_conv_b[l]
    return dict(
        cwx=cw[:, 0:2048], cbx=cb[None, 0:2048], cwb=cw[:, 2048:2560], cbb=cb[None, 2048:2560],
        cwc=cw[:, 2560:3072], cbc=cb[None, 2560:3072],
        dtb_row=ssd_dt_bias[l][None, :], dtb_col=ssd_dt_bias[l][:, None],
        alog_row=ssd_a_log[l][None, :], alog_col=ssd_a_log[l][:, None],
        dfull=jnp.repeat(ssd_d[l], SSD_HEAD_DIM)[None, :], ssd_norm=ssd_norm[l][None, :],
        gm_norm=gm_norm[l][None, :], gm_ws=gm_ws[l], gm_bst=gm_bs[l].T,
        gm_w0=jnp.repeat(gm_ws[l][:, 0, 0], GM_GROUP_DIM)[None, :],
        gm_b0=jnp.repeat(gm_bs[l][:, 0], GM_GROUP_DIM)[None, :],
        ib_row=m_i_bias[l][None, :], ib_col=m_i_bias[l][:, None],
        fb_row=m_f_bias[l][None, :], fb_col=m_f_bias[l][:, None], m_norm=m_norm[l][None, :],
        ffn_cw=ffn_conv_w[l], ffn_cb=ffn_conv_b[l][None, :],
    )


def kernel(x_prompt, x_sample, state_ssd, state_ssd_conv, state_mlstm_c, state_mlstm_n, state_mlstm_m,
           state_ffn_conv, norm1, w_in, ssd_conv_w, ssd_conv_b, ssd_dt_bias, ssd_a_log, ssd_d, ssd_norm,
           gm_norm, gm_ws, gm_bs, m_i_bias, m_f_bias, m_norm, w_branch, w_out, norm2,
           w_gate, w_up, ffn_conv_w, ffn_conv_b, w_down, final_norm):
    n_batch, seq, d = x_prompt.shape
    n_dec = x_sample.shape[0]
    depth = w_in.shape[0]
    n_chunks = seq // CHUNK
    n_prompt = n_batch * seq
    n_rows = n_prompt + n_dec
    tm = _pick_tile(n_rows, (832, 640, 384, 128))
    tm_big = _pick_tile(n_rows, (1664, 832, 640, 384, 128))

    ssd_conv_all = state_ssd_conv.reshape(depth, n_dec, -1)
    ssd_h_all = state_ssd.reshape(depth, n_dec, D_MODEL, SSD_STATE)
    ml_n_all = state_mlstm_n.reshape(depth, n_dec, -1)
    ffn_conv_all = state_ffn_conv.reshape(depth, n_dec, -1)
    s_ssd = None
    s_ml = None

    def xbc(rows):
        c0 = PROJ_SRC["x"][1]
        return rows[..., c0:c0 + 3072]

    def tail_rows(a, n_tail):
        return jnp.stack([a[(b + 1) * seq - n_tail:(b + 1) * seq] for b in range(n_batch)])

    x = jnp.concatenate([x_prompt.reshape(n_prompt, d), x_sample.reshape(n_dec, d)], axis=0)
    outs = [[] for _ in range(13)]
    for l in range(depth):
        p = _layer_params(l, ssd_conv_w, ssd_conv_b, ssd_dt_bias, ssd_a_log, ssd_d, ssd_norm, gm_norm, gm_ws, gm_bs,
                          m_i_bias, m_f_bias, m_norm, ffn_conv_w, ffn_conv_b)
        h = _rmsnorm_rows(x, norm1[l], BF16, tm)
        proj = {"a": _matmul(h, w_in, l, tm_big, 512, src=W_IN_A),
                "b": _matmul(h, w_in, l, tm_big, 512, src=W_IN_B),
                "g": _matmul(h, w_in, l, tm_big, 512, src=W_IN_G),
                "s": _matmul_small(h, w_in, l, tm_big)}

        y0, p_ssd = _ssd_prompt(proj, n_batch, n_chunks, n_rows, p)
        y1 = _gm_prompt(proj, n_prompt, n_rows, p)
        y2, p_c, p_n, p_m = _ml_prompt(proj, n_batch, n_chunks, n_rows, p)

        ys0, s_ssd = _ssd_sample(proj, n_prompt, n_dec, l, ssd_conv_all, ssd_h_all, p, s_ssd)
        ys2, *s_ml = _ml_sample(proj, n_prompt, n_dec, l, state_mlstm_c, ml_n_all, state_mlstm_m, p, s_ml)
        y0, y1, y2, s_vn = _gm_sample_pack(proj, n_prompt, n_dec, p, ys0, ys2, y0, y1, y2)

        merged = _merge(y0, y1, y2, w_branch, l, proj, tm, 256)
        x = _matmul(merged, w_out, l, tm_big, 512, residual=x)

        h2 = _rmsnorm_rows(x, norm2[l], BF16, tm)
        g = _matmul(h2, w_gate, l, tm_big, 512)
        u = _matmul(h2, w_up, l, tm_big, 512)
        act = _ffn_act_prompt(g, u, n_batch, seq, n_rows, p)
        act = _ffn_act_sample(g, u, n_prompt, n_dec, l, ffn_conv_all, p, act)
        x = _matmul(act, w_down, l, tm, 512, residual=x)

        outs[0].append(p_ssd.reshape(n_batch, SSD_HEADS, SSD_HEAD_DIM, SSD_STATE))
        outs[1].append(xbc(tail_rows(proj["a"], 3)))
        outs[2].append(p_c)
        outs[3].append(p_n)
        outs[4].append(p_m[:, :, 0])
        outs[5].append(tail_rows(g, 2))
        outs[7].append(jnp.concatenate([state_ssd_conv[l][:, 1:], xbc(proj["a"][n_prompt:])[:, None, :]], axis=1))
        outs[11].append(jnp.concatenate([state_ffn_conv[l][:, 1:], g[n_prompt:, None, :]], axis=1))
        outs[12].append(s_vn.reshape(n_dec, 1, D_MODEL))

    tf = _pick_tile(n_prompt, (1024, 512, 256, 128))
    y_prompt = _rmsnorm_rows(x, final_norm, F32, tf, 0, n_prompt).reshape(n_batch, seq, d)
    y_sample = _rmsnorm_rows(x, final_norm, F32, n_dec, n_prompt // n_dec, n_dec).reshape(n_dec, 1, d)
    s_c, s_n, s_m = s_ml
    stacked = {6: s_ssd.reshape(depth, n_dec, SSD_HEADS, SSD_HEAD_DIM, SSD_STATE), 8: s_c,
               9: s_n.reshape(depth, n_dec, M_HEADS, M_DQK), 10: s_m}
    return (y_prompt, y_sample) + tuple(stacked[i] if i in stacked else jnp.stack(outs[i]) for i in range(13))
```

```python
import functools

import jax
import jax.numpy as jnp
from jax import lax
from jax.experimental import pallas as pl
from jax.experimental.pallas import tpu as pltpu

F32 = jnp.float32
BF16 = jnp.bfloat16

D_MODEL = 2048
DEPTH = 4
CHUNK = 128
EPS = 1e-6
SSD_HEADS = 32
SSD_HEAD_DIM = 64
SSD_GROUPS = 4
SSD_STATE = 128
SSD_GROUP_COLS = D_MODEL // SSD_GROUPS
GM_GROUPS = 8
GM_GROUP_DIM = 256
M_HEADS = 8
M_DV = 256
M_DQK = 128
D_FF = 5632

PROJ_SRC = {"z": ("a", 0), "x": ("a", 2048), "B": ("a", 4096), "C": ("a", 4608),
            "u": ("b", 0), "vg": ("b", 2048), "q": ("b", 4096), "k": ("b", 5120), "v": ("b", 6144), "o": ("b", 8192),
            "g": ("g", 0), "s": ("s", 0)}
W_IN_A = (0, 5120)
W_IN_B = (5152, 10240)
W_IN_G = (15408, 6144)
W_IN_DT_COL = 5120
W_IN_IF_COL = 15392
SM_W = 256
S_DT = 0
S_IG = 128
S_FG = 136

VMEM_LIMIT_BYTES = 56 * 1024 * 1024
SAMPLE_ROWS_PER_STEP = 8


def _cp(*sem):
    return pltpu.CompilerParams(dimension_semantics=sem, vmem_limit_bytes=VMEM_LIMIT_BYTES)


def _dot(a, b):
    return jnp.dot(a, b, preferred_element_type=F32)


def _dot_rt(a, b):
    return lax.dot_general(a, b, (((1,), (1,)), ((), ())), preferred_element_type=F32)


def _dot_lt(a, b):
    return lax.dot_general(a, b, (((0,), (0,)), ((), ())), preferred_element_type=F32)


def _split3(a):
    a1 = a.astype(BF16)
    r1 = a - a1.astype(F32)
    a2 = r1.astype(BF16)
    a3 = (r1 - a2.astype(F32)).astype(BF16)
    return a1, a2, a3


def _dot3_l(a_f32, b_bf16):
    a1, a2, a3 = _split3(a_f32)
    return (_dot(a1, b_bf16) + _dot(a2, b_bf16)) + _dot(a3, b_bf16)


def _dot3_r(a_bf16, b_f32):
    b1, b2, b3 = _split3(b_f32)
    return (_dot(a_bf16, b1) + _dot(a_bf16, b2)) + _dot(a_bf16, b3)


def _tri(n, lower=True):
    r = lax.broadcasted_iota(jnp.int32, (n, n), 0)
    c = lax.broadcasted_iota(jnp.int32, (n, n), 1)
    return (r >= c) if lower else (r <= c)


def _softplus(x):
    return jnp.maximum(x, 0.0) + jnp.log1p(jnp.exp(-jnp.abs(x)))


def _log_sigmoid(x):
    return -_softplus(-x)


def _silu(x):
    return x * jax.nn.sigmoid(x)


def _rms_kernel(x_ref, g_ref, o_ref):
    x = x_ref[...]
    y = x * lax.rsqrt(jnp.mean(x * x, axis=-1, keepdims=True) + EPS)
    o_ref[...] = (y * g_ref[...]).astype(o_ref.dtype)


def _rmsnorm_rows(x, g, out_dtype, tm, row_block0=0, n_rows=None):
    m_total, d = x.shape
    n_rows = m_total if n_rows is None else n_rows
    return pl.pallas_call(
        _rms_kernel,
        grid=(n_rows // tm,),
        in_specs=[pl.BlockSpec((tm, d), lambda i: (row_block0 + i, 0)),
                  pl.BlockSpec((1, d), lambda i: (0, 0))],
        out_specs=pl.BlockSpec((tm, d), lambda i: (i, 0)),
        out_shape=jax.ShapeDtypeStruct((n_rows, d), out_dtype),
        compiler_params=_cp("parallel"),
        name="rmsnorm_rows",
    )(x, g.reshape(1, d))


def _mm_kernel(has_res, *refs):
    a_ref, w_ref = refs[0], refs[1]
    acc = _dot(a_ref[...], w_ref[...].astype(BF16))
    if has_res:
        acc = refs[2][...] + acc
    o_ref = refs[-1]
    o_ref[...] = acc.astype(o_ref.dtype)


def _lhs_spec(tm, k):
    return pl.BlockSpec((tm, k), lambda i, j: (i, 0), pipeline_mode=pl.Buffered(1))


def _matmul(a, w, layer, tm, tn, residual=None):
    m, k = a.shape
    n = w.shape[-1]
    in_specs = [_lhs_spec(tm, k),
                pl.BlockSpec((None, k, tn), lambda i, j: (layer, 0, j))]
    args = [a, w]
    if residual is not None:
        in_specs.append(pl.BlockSpec((tm, tn), lambda i, j: (i, j)))
        args.append(residual)
    return pl.pallas_call(
        functools.partial(_mm_kernel, residual is not None),
        grid=(m // tm, n // tn),
        in_specs=in_specs,
        out_specs=pl.BlockSpec((tm, tn), lambda i, j: (i, j)),
        out_shape=jax.ShapeDtypeStruct((m, n), F32),
        compiler_params=_cp("parallel", "arbitrary"),
        name="matmul_res" if residual is not None else "matmul",
    )(*args)


def _mm_t_kernel(a_ref, wt_ref, o_ref):
    o_ref[...] = _dot_rt(a_ref[...], wt_ref[0].astype(BF16))


def _wt_spec(layer, rows, k, row_of_step):
    def index_map(*idx):
        r = row_of_step(*idx)
        return (layer, r if isinstance(r, int) else pl.multiple_of(r, 8), 0)

    return pl.BlockSpec((pl.Element(1), pl.Element(rows), pl.Element(k)), index_map)


def _matmul_t(a, wt, layer, tm, tn, col0, n):
    m, k = a.shape
    return pl.pallas_call(
        _mm_t_kernel,
        grid=(m // tm, n // tn),
        in_specs=[_lhs_spec(tm, k),
                  _wt_spec(layer, tn, k, lambda i, j: col0 + tn * j)],
        out_specs=pl.BlockSpec((tm, tn), lambda i, j: (i, j)),
        out_shape=jax.ShapeDtypeStruct((m, n), F32),
        compiler_params=_cp("parallel", "arbitrary"),
        name="matmul_t",
    )(a, wt)


def _mm_small_kernel(a_ref, wdt_ref, wif_ref, o_ref):
    a = a_ref[...]
    o_ref[...] = jnp.zeros(o_ref.shape, F32)
    o_ref[:, S_DT:S_DT + SSD_HEADS] = _dot_rt(a, wdt_ref[0].astype(BF16))
    o_ref[:, S_IG:S_IG + 2 * M_HEADS] = _dot_rt(a, wif_ref[0].astype(BF16))


def _matmul_small(a, wt, layer, tm):
    m, k = a.shape
    return pl.pallas_call(
        _mm_small_kernel,
        grid=(m // tm,),
        in_specs=[pl.BlockSpec((tm, k), lambda i: (i, 0)),
                  _wt_spec(layer, SSD_HEADS, k, lambda i: W_IN_DT_COL),
                  _wt_spec(layer, 2 * M_HEADS, k, lambda i: W_IN_IF_COL)],
        out_specs=pl.BlockSpec((tm, SM_W), lambda i: (i, 0)),
        out_shape=jax.ShapeDtypeStruct((m, SM_W), F32),
        compiler_params=_cp("parallel"),
        name="matmul_small",
    )(a, wt, wt)


def _merge_kernel(y0_ref, y1_ref, y2_ref, w_ref, g0_ref, g1_ref, g2_ref, o_ref):
    acc = jax.nn.sigmoid(g0_ref[...]) * _dot(y0_ref[...], w_ref[0].astype(BF16))
    acc = acc + jax.nn.sigmoid(g1_ref[...]) * _dot(y1_ref[...], w_ref[1].astype(BF16))
    acc = acc + jax.nn.sigmoid(g2_ref[...]) * _dot(y2_ref[...], w_ref[2].astype(BF16))
    o_ref[...] = acc.astype(o_ref.dtype)


def _merge(y0, y1, y2, w_branch, layer, proj, tm, tn):
    m = y0.shape[0]
    gb = 0
    gstep = D_MODEL // tn
    yspec = pl.BlockSpec((tm, D_MODEL), lambda i, j: (i, 0))

    def gspec(b):
        return pl.BlockSpec((tm, tn), lambda i, j: (i, gb + b * gstep + j))

    return pl.pallas_call(
        _merge_kernel,
        grid=(m // tm, D_MODEL // tn),
        in_specs=[yspec, yspec, yspec,
                  pl.BlockSpec((None, 3, D_MODEL, tn), lambda i, j: (layer, 0, 0, j)),
                  gspec(0), gspec(1), gspec(2)],
        out_specs=pl.BlockSpec((tm, tn), lambda i, j: (i, j)),
        out_shape=jax.ShapeDtypeStruct((m, D_MODEL), BF16),
        compiler_params=_cp("parallel", "arbitrary"),
        name="merge",
    )(y0, y1, y2, w_branch, proj["g"], proj["g"], proj["g"])


def _pair_select(lane_lo, col_a, col_b):
    return jnp.where(lane_lo, col_a, col_b)


def _ssd_prompt_kernel(z_ref, x_ref, b_ref, c_ref, sm_ref,
                       cwx_ref, cbx_ref, cwb_ref, cbb_ref, cwc_ref, cbc_ref,
                       dtb_row_ref, dtb_col_ref, alog_row_ref, alog_col_ref, dfull_ref, norm_ref,
                       y_ref, hout_ref,
                       xpad, bpad, cpad, ht_scr, ybuf):
    c = pl.program_id(1)
    nc = pl.num_programs(1)
    q = CHUNK

    @pl.when(c == 0)
    def _():
        xpad[0:8, :] = jnp.zeros((8, xpad.shape[1]), F32)
        bpad[0:8, :] = jnp.zeros((8, bpad.shape[1]), F32)
        cpad[0:8, :] = jnp.zeros((8, cpad.shape[1]), F32)
        ht_scr[...] = jnp.zeros(ht_scr.shape, F32)

    @pl.when(c > 0)
    def _():
        xpad[0:8, :] = xpad[q:q + 8, :]
        bpad[0:8, :] = bpad[q:q + 8, :]
        cpad[0:8, :] = cpad[q:q + 8, :]

    xpad[8:q + 8, :] = x_ref[...]
    bpad[8:q + 8, :] = b_ref[...]
    cpad[8:q + 8, :] = c_ref[...]

    def conv(pad, w_ref, bias_ref, lanes):
        acc = pad[5:q + 5, lanes] * w_ref[0:1, lanes]
        acc = acc + pad[6:q + 6, lanes] * w_ref[1:2, lanes]
        acc = acc + pad[7:q + 7, lanes] * w_ref[2:3, lanes]
        acc = acc + pad[8:q + 8, lanes] * w_ref[3:4, lanes]
        return _silu(acc + bias_ref[:, lanes])

    bm = conv(bpad, cwb_ref, cbb_ref, slice(None))
    cm = conv(cpad, cwc_ref, cbc_ref, slice(None))

    sm = sm_ref[...]
    sm_t = sm.T
    dt_col = _softplus(sm[:, S_DT:S_DT + SSD_HEADS] + dtb_row_ref[...])
    dt_row = _softplus(sm_t[S_DT:S_DT + SSD_HEADS, :] + dtb_col_ref[...])
    da_col = dt_col * (-jnp.exp(alog_row_ref[...]))
    da_row = dt_row * (-jnp.exp(alog_col_ref[...]))
    tril = _tri(q, True)
    s_col = _dot3_r(tril.astype(BF16), da_col)
    s_row = _dot3_l(da_row, _tri(q, False).astype(BF16))
    es_col = jnp.exp(s_col)
    s_last = s_col[q - 1:q, :]
    wend_col = jnp.exp(s_last - s_col) * dt_col
    elast = jnp.exp(s_last)

    lane_lo = lax.broadcasted_iota(jnp.int32, (q, 128), 1) < SSD_HEAD_DIM
    lane_lo1 = lax.broadcasted_iota(jnp.int32, (1, 128), 1) < SSD_HEAD_DIM

    for g in range(SSD_GROUPS):
        bg = bm[:, g * SSD_STATE:(g + 1) * SSD_STATE]
        cg = cm[:, g * SSD_STATE:(g + 1) * SSD_STATE].astype(BF16)
        bg_t = bg.T.astype(BF16)
        cb = _dot(cg, bg_t)
        gsl = slice(g * SSD_GROUP_COLS, (g + 1) * SSD_GROUP_COLS)
        yint = _dot(cg, ht_scr[:, gsl].astype(BF16))
        for jj in range(4):
            j = g * 4 + jj
            ha, hb = 2 * j, 2 * j + 1
            psl = slice(j * 128, (j + 1) * 128)
            xp = conv(xpad, cwx_ref, cbx_ref, psl)
            xp_bf = xp.astype(BF16)
            ys = []
            for h in (ha, hb):
                dec = jnp.exp(jnp.where(tril, s_col[:, h:h + 1] - s_row[h:h + 1, :], -jnp.inf))
                w = cb * dec * dt_row[h:h + 1, :]
                ys.append(_dot(w.astype(BF16), xp_bf))
            y = jnp.where(lane_lo, ys[0], ys[1])
            y = y + yint[:, jj * 128:(jj + 1) * 128] * _pair_select(lane_lo, es_col[:, ha:ha + 1], es_col[:, hb:hb + 1])
            y = y + dfull_ref[:, psl] * xp
            ybuf[:, psl] = y * _silu(z_ref[:, psl])
            xw = xp * _pair_select(lane_lo, wend_col[:, ha:ha + 1], wend_col[:, hb:hb + 1])
            dpair = _pair_select(lane_lo1, elast[:, ha:ha + 1], elast[:, hb:hb + 1])
            ht_scr[:, psl] = ht_scr[:, psl] * dpair + _dot(bg_t, xw.astype(BF16))
        yg = ybuf[:, gsl]
        yn = yg * lax.rsqrt(jnp.mean(yg * yg, axis=-1, keepdims=True) + EPS)
        y_ref[:, gsl] = (yn * norm_ref[:, gsl]).astype(y_ref.dtype)

    @pl.when(c == nc - 1)
    def _():
        hout_ref[...] = ht_scr[...].T


def _proj_arrays(proj, names):
    return [proj[PROJ_SRC[n][0]] for n in names]


def _ssd_prompt(proj, n_batch, n_chunks, n_rows_total, p):
    q = CHUNK
    rb = lambda b, c: b * n_chunks + c

    def pspec(width, name):
        return pl.BlockSpec((q, width), lambda b, c: (rb(b, c), PROJ_SRC[name][1] // width))

    def wspec(shape):
        return pl.BlockSpec(shape, lambda b, c: (0,) * len(shape))

    return pl.pallas_call(
        _ssd_prompt_kernel,
        grid=(n_batch, n_chunks),
        in_specs=[pspec(2048, "z"), pspec(2048, "x"), pspec(512, "B"), pspec(512, "C"), pspec(SM_W, "s"),
                  wspec((4, 2048)), wspec((1, 2048)), wspec((4, 512)), wspec((1, 512)), wspec((4, 512)), wspec((1, 512)),
                  wspec((1, 32)), wspec((32, 1)), wspec((1, 32)), wspec((32, 1)), wspec((1, 2048)), wspec((1, 2048))],
        out_specs=[pl.BlockSpec((q, D_MODEL), lambda b, c: (rb(b, c), 0)),
                   pl.BlockSpec((None, D_MODEL, SSD_STATE), lambda b, c: (b, 0, 0))],
        out_shape=[jax.ShapeDtypeStruct((n_rows_total, D_MODEL), BF16),
                   jax.ShapeDtypeStruct((n_batch, D_MODEL, SSD_STATE), F32)],
        scratch_shapes=[pltpu.VMEM((q + 8, 2048), F32), pltpu.VMEM((q + 8, 512), F32), pltpu.VMEM((q + 8, 512), F32),
                        pltpu.VMEM((SSD_STATE, D_MODEL), F32), pltpu.VMEM((q, D_MODEL), F32)],
        compiler_params=_cp("parallel", "arbitrary"),
        name="ssd_prompt",
    )(*_proj_arrays(proj, "zxBCs"),
      p["cwx"], p["cbx"], p["cwb"], p["cbb"], p["cwc"], p["cbc"],
      p["dtb_row"], p["dtb_col"], p["alog_row"], p["alog_col"], p["dfull"], p["ssd_norm"])


def _ssd_sample_kernel(z_ref, x_ref, b_ref, c_ref, sm_ref, cs0_ref, cs1_ref, cs2_ref,
                       cwx_ref, cbx_ref, cwb_ref, cbb_ref, cwc_ref, cbc_ref,
                       dtb_row_ref, alog_row_ref, dfull_ref, norm_ref, hin_ref,
                       y_ref, hout_ref):
    r8 = SAMPLE_ROWS_PER_STEP

    def conv(lo, hi, new, w_ref, bias_ref):
        acc = cs0_ref[:, lo:hi] * w_ref[0:1, :]
        acc = acc + cs1_ref[:, lo:hi] * w_ref[1:2, :]
        acc = acc + cs2_ref[:, lo:hi] * w_ref[2:3, :]
        acc = acc + new * w_ref[3:4, :]
        return _silu(acc + bias_ref[...])

    xs = conv(0, 2048, x_ref[...], cwx_ref, cbx_ref)
    bm = conv(2048, 2560, b_ref[...], cwb_ref, cbb_ref)
    cm = conv(2560, 3072, c_ref[...], cwc_ref, cbc_ref)
    dt = _softplus(sm_ref[:, S_DT:S_DT + SSD_HEADS] + dtb_row_ref[...])
    e = jnp.exp(dt * (-jnp.exp(alog_row_ref[...])))
    hh = lax.broadcasted_iota(jnp.int32, (SSD_HEADS, D_MODEL), 0)
    cc = lax.broadcasted_iota(jnp.int32, (SSD_HEADS, D_MODEL), 1)
    expand = jnp.where((cc >= hh * SSD_HEAD_DIM) & (cc < (hh + 1) * SSD_HEAD_DIM), 1.0, 0.0).astype(BF16)
    dt_full = _dot3_l(dt, expand)
    e_full = _dot3_l(e, expand)
    dx = xs * dt_full
    e1, e2, e3 = (t.astype(F32) for t in _split3(e_full))
    row = lax.broadcasted_iota(jnp.int32, (r8, SSD_GROUP_COLS), 0)
    row_n = lax.broadcasted_iota(jnp.int32, (r8, SSD_STATE), 0)
    ones = jnp.ones((r8, SSD_STATE), BF16)

    ygroups = []
    for g in range(SSD_GROUPS):
        gsl = slice(g * SSD_GROUP_COLS, (g + 1) * SSD_GROUP_COLS)
        nsl = slice(g * SSD_STATE, (g + 1) * SSD_STATE)
        bg = bm[:, nsl].astype(BF16)
        cg = cm[:, nsl]
        dxg = dx[:, gsl]
        yacc = jnp.zeros((r8, SSD_GROUP_COLS), F32)
        for r in range(r8):
            a_e = jnp.where(row == 0, e1[r:r + 1, gsl],
                            jnp.where(row == 1, e2[r:r + 1, gsl],
                                      jnp.where(row == 2, e3[r:r + 1, gsl], 0.0))).astype(BF16)
            ecol = _dot_lt(a_e, ones)
            a_x = jnp.where(row == r, dxg, 0.0).astype(BF16)
            hn = hin_ref[r, gsl, :] * ecol + _dot_lt(a_x, bg)
            hout_ref[r, gsl, :] = hn
            c_r = jnp.where(row_n == r, cg, 0.0).astype(BF16)
            yacc = yacc + _dot_rt(c_r, hn.astype(BF16))
        ygroups.append(yacc)

    for g in range(SSD_GROUPS):
        gsl = slice(g * SSD_GROUP_COLS, (g + 1) * SSD_GROUP_COLS)
        y = ygroups[g] + dfull_ref[:, gsl] * xs[:, gsl]
        y = y * _silu(z_ref[:, gsl])
        yn = y * lax.rsqrt(jnp.mean(y * y, axis=-1, keepdims=True) + EPS)
        y_ref[:, gsl] = yn * norm_ref[:, gsl]


def _drop_refs(kernel_fn, n_inputs, n_dropped):
    def body(*refs):
        return kernel_fn(*refs[:n_inputs], *refs[n_inputs + n_dropped:])
    return body


def _ssd_sample(proj, row0, n_dec, layer, conv_state_all, h_state_all, p, h_out_prev):
    r8 = SAMPLE_ROWS_PER_STEP
    rb0 = row0 // r8
    depth = h_state_all.shape[0]

    def pspec(width, name):
        return pl.BlockSpec((r8, width), lambda i: (rb0 + i, PROJ_SRC[name][1] // width))

    def wspec(shape):
        return pl.BlockSpec(shape, lambda i: (0,) * len(shape))

    def cspec(j):
        return pl.BlockSpec((None, r8, 3072), lambda i: (layer, i, j))

    hspec = pl.BlockSpec((None, r8, D_MODEL, SSD_STATE), lambda i: (layer, i, 0, 0))
    in_specs = [pspec(2048, "z"), pspec(2048, "x"), pspec(512, "B"), pspec(512, "C"), pspec(SM_W, "s"),
                cspec(0), cspec(1), cspec(2),
                wspec((4, 2048)), wspec((1, 2048)), wspec((4, 512)), wspec((1, 512)), wspec((4, 512)), wspec((1, 512)),
                wspec((1, 32)), wspec((1, 32)), wspec((1, 2048)), wspec((1, 2048)), hspec]
    args = [*_proj_arrays(proj, "zxBCs"), conv_state_all, conv_state_all, conv_state_all,
            p["cwx"], p["cbx"], p["cwb"], p["cbb"], p["cwc"], p["cbc"],
            p["dtb_row"], p["alog_row"], p["dfull"], p["ssd_norm"], h_state_all]
    n_in = len(args)
    aliases = {}
    if h_out_prev is not None:
        in_specs.append(pl.BlockSpec(memory_space=pl.ANY))
        args.append(h_out_prev)
        aliases = {n_in: 1}
    return pl.pallas_call(
        _drop_refs(_ssd_sample_kernel, n_in, len(aliases)),
        grid=(n_dec // r8,),
        in_specs=in_specs,
        out_specs=[pl.BlockSpec((r8, D_MODEL), lambda i: (i, 0)), hspec],
        out_shape=[jax.ShapeDtypeStruct((n_dec, D_MODEL), F32),
                   jax.ShapeDtypeStruct((depth, n_dec, D_MODEL, SSD_STATE), F32)],
        input_output_aliases=aliases,
        compiler_params=_cp("parallel"),
        name="ssd_sample",
    )(*args)


GM_ROWS_PER_STEP = 256


def _gm_prompt_kernel(u_ref, v_ref, gnorm_ref, ws_ref, bst_ref, y_ref):
    q = CHUNK
    tril = _tri(q, True)
    for cc in range(GM_ROWS_PER_STEP // q):
        rsl = slice(cc * q, (cc + 1) * q)
        u = jax.nn.gelu(u_ref[rsl, :])
        v = jax.nn.gelu(v_ref[rsl, :])
        vn = v * lax.rsqrt(jnp.mean(v * v, axis=-1, keepdims=True) + EPS) * gnorm_ref[...]
        for g in range(GM_GROUPS):
            gsl = slice(g * GM_GROUP_DIM, (g + 1) * GM_GROUP_DIM)
            w = jnp.where(tril, ws_ref[g], 0.0).astype(BF16)
            mixed = _dot(w, vn[:, gsl].astype(BF16)) + bst_ref[:, g:g + 1]
            y_ref[rsl, gsl] = (u[:, gsl] * mixed).astype(y_ref.dtype)


def _gm_prompt(proj, n_prompt_rows, n_rows_total, p):
    r = GM_ROWS_PER_STEP
    return pl.pallas_call(
        _gm_prompt_kernel,
        grid=(n_prompt_rows // r,),
        in_specs=[pl.BlockSpec((r, 2048), lambda i: (i, PROJ_SRC["u"][1] // 2048)),
                  pl.BlockSpec((r, 2048), lambda i: (i, PROJ_SRC["vg"][1] // 2048)),
                  pl.BlockSpec((1, 2048), lambda i: (0, 0)),
                  pl.BlockSpec((GM_GROUPS, CHUNK, CHUNK), lambda i: (0, 0, 0)),
                  pl.BlockSpec((CHUNK, GM_GROUPS), lambda i: (0, 0))],
        out_specs=pl.BlockSpec((r, D_MODEL), lambda i: (i, 0)),
        out_shape=jax.ShapeDtypeStruct((n_rows_total, D_MODEL), BF16),
        compiler_params=_cp("parallel"),
        name="gm_prompt",
    )(proj["b"], proj["b"], p["gm_norm"], p["gm_ws"], p["gm_bst"])


def _gm_sample_pack_kernel(u_ref, v_ref, gnorm_ref, w0_ref, b0_ref, yssd_ref, ym_ref,
                           y0_any, y1_any, y2_any,
                           o0_ref, o1_ref, o2_ref, vn_ref):
    del y0_any, y1_any, y2_any
    u = jax.nn.gelu(u_ref[...])
    v = jax.nn.gelu(v_ref[...])
    vn = v * lax.rsqrt(jnp.mean(v * v, axis=-1, keepdims=True) + EPS) * gnorm_ref[...]
    vn_ref[...] = vn
    mixed = w0_ref[...] * vn + b0_ref[...]
    o0_ref[...] = yssd_ref[...].astype(o0_ref.dtype)
    o1_ref[...] = (u * mixed).astype(o1_ref.dtype)
    o2_ref[...] = ym_ref[...].astype(o2_ref.dtype)


def _gm_sample_pack(proj, row0, n_dec, p, yssd_s, ym_s, y0, y1, y2):
    rb = row0 // n_dec
    full = lambda shape: pl.BlockSpec(shape, lambda i: (0,) * len(shape))
    anyspec = pl.BlockSpec(memory_space=pl.ANY)
    ospec = pl.BlockSpec((n_dec, D_MODEL), lambda i: (rb, 0))
    return pl.pallas_call(
        _gm_sample_pack_kernel,
        grid=(1,),
        in_specs=[pl.BlockSpec((n_dec, 2048), lambda i: (rb, PROJ_SRC["u"][1] // 2048)),
                  pl.BlockSpec((n_dec, 2048), lambda i: (rb, PROJ_SRC["vg"][1] // 2048)),
                  full((1, 2048)), full((1, 2048)), full((1, 2048)),
                  full((n_dec, D_MODEL)), full((n_dec, D_MODEL)),
                  anyspec, anyspec, anyspec],
        out_specs=[ospec, ospec, ospec, full((n_dec, D_MODEL))],
        out_shape=[jax.ShapeDtypeStruct(y0.shape, y0.dtype), jax.ShapeDtypeStruct(y1.shape, y1.dtype),
                   jax.ShapeDtypeStruct(y2.shape, y2.dtype), jax.ShapeDtypeStruct((n_dec, D_MODEL), F32)],
        input_output_aliases={7: 0, 8: 1, 9: 2},
        compiler_params=_cp("arbitrary"),
        name="gm_sample_pack",
    )(proj["b"], proj["b"], p["gm_norm"], p["gm_w0"], p["gm_b0"], yssd_s, ym_s, y0, y1, y2)


def _ml_prompt_kernel(q_ref, k_ref, v_ref, o_ref, sm_ref,
                      ib_row_ref, ib_col_ref, fb_row_ref, fb_col_ref, mnorm_ref,
                      y_ref, c_ref, n_ref, m_ref):
    c = pl.program_id(1)
    q = CHUNK

    @pl.when(c == 0)
    def _():
        c_ref[...] = jnp.zeros(c_ref.shape, F32)
        n_ref[...] = jnp.zeros(n_ref.shape, F32)
        m_ref[...] = jnp.zeros(m_ref.shape, F32)

    sm = sm_ref[...]
    sm_t = sm.T
    li_col = sm[:, S_IG:S_IG + M_HEADS] + ib_row_ref[...]
    lf_col = _log_sigmoid(sm[:, S_FG:S_FG + M_HEADS] + fb_row_ref[...])
    li_row = sm_t[S_IG:S_IG + M_HEADS, :] + ib_col_ref[...]
    lf_row = _log_sigmoid(sm_t[S_FG:S_FG + M_HEADS, :] + fb_col_ref[...])
    tril = _tri(q, True)
    b_col = _dot3_r(tril.astype(BF16), lf_col)
    b_row = _dot3_l(lf_row, _tri(q, False).astype(BF16))
    b_last = b_col[q - 1:q, :]

    for h in range(M_HEADS):
        ksl = slice(h * M_DQK, (h + 1) * M_DQK)
        vsl = slice(h * M_DV, (h + 1) * M_DV)
        m_prev = m_ref[h:h + 1, 0:1]
        bc = b_col[:, h:h + 1]
        log_d = jnp.where(tril, bc - b_row[h:h + 1, :] + li_row[h:h + 1, :], -jnp.inf)
        inter = bc + m_prev
        s = jnp.maximum(inter, jnp.max(log_d, axis=1, keepdims=True))
        w_inter = jnp.exp(inter - s)
        qh = q_ref[:, ksl]
        qh_bf = qh.astype(BF16)
        kh = k_ref[:, ksl] * (M_DQK ** -0.5)
        vh_bf = v_ref[:, vsl].astype(BF16)
        qk = _dot_rt(qh_bf, kh.astype(BF16)) * jnp.exp(log_d - s)
        c_prev = c_ref[h]
        n_prev = n_ref[h:h + 1, :]
        num = _dot(qk.astype(BF16), vh_bf) + w_inter * _dot(qh_bf, c_prev.astype(BF16))
        den = jnp.sum(qk, axis=1, keepdims=True) + w_inter * jnp.sum(qh * n_prev, axis=1, keepdims=True)
        hm = num / jnp.maximum(jnp.abs(den), jnp.exp(-s))
        hn = hm * lax.rsqrt(jnp.mean(hm * hm, axis=-1, keepdims=True) + EPS) * mnorm_ref[:, vsl]
        y_ref[:, vsl] = (hn * jax.nn.sigmoid(o_ref[:, vsl])).astype(y_ref.dtype)

        m_new = s[q - 1:q, :]
        wk = jnp.exp(b_last[:, h:h + 1] - bc + li_col[:, h:h + 1] - m_new)
        decay = jnp.exp(b_last[:, h:h + 1] + m_prev - m_new)
        kw = kh * wk
        c_ref[h] = decay * c_prev + _dot_lt(kw.astype(BF16), vh_bf)
        n_ref[h:h + 1, :] = decay * n_prev + jnp.sum(kw, axis=0, keepdims=True)
        m_ref[h:h + 1, :] = jnp.broadcast_to(m_new, (1, m_ref.shape[1]))


def _ml_prompt(proj, n_batch, n_chunks, n_rows_total, p):
    q = CHUNK
    rb = lambda b, c: b * n_chunks + c

    def pspec(width, name):
        return pl.BlockSpec((q, width), lambda b, c: (rb(b, c), PROJ_SRC[name][1] // width))

    def wspec(shape):
        return pl.BlockSpec(shape, lambda b, c: (0,) * len(shape))

    return pl.pallas_call(
        _ml_prompt_kernel,
        grid=(n_batch, n_chunks),
        in_specs=[pspec(1024, "q"), pspec(1024, "k"), pspec(2048, "v"), pspec(2048, "o"), pspec(SM_W, "s"),
                  wspec((1, 8)), wspec((8, 1)), wspec((1, 8)), wspec((8, 1)), wspec((1, 2048))],
        out_specs=[pl.BlockSpec((q, D_MODEL), lambda b, c: (rb(b, c), 0)),
                   pl.BlockSpec((None, M_HEADS, M_DQK, M_DV), lambda b, c: (b, 0, 0, 0)),
                   pl.BlockSpec((None, M_HEADS, M_DQK), lambda b, c: (b, 0, 0)),
                   pl.BlockSpec((None, M_HEADS, 128), lambda b, c: (b, 0, 0))],
        out_shape=[jax.ShapeDtypeStruct((n_rows_total, D_MODEL), BF16),
                   jax.ShapeDtypeStruct((n_batch, M_HEADS, M_DQK, M_DV), F32),
                   jax.ShapeDtypeStruct((n_batch, M_HEADS, M_DQK), F32),
                   jax.ShapeDtypeStruct((n_batch, M_HEADS, 128), F32)],
        compiler_params=_cp("parallel", "arbitrary"),
        name="ml_prompt",
    )(*_proj_arrays(proj, "qkvos"),
      p["ib_row"], p["ib_col"], p["fb_row"], p["fb_col"], p["m_norm"])


def _ml_sample_kernel(q_ref, k_ref, v_ref, o_ref, sm_ref, ib_row_ref, fb_row_ref, mnorm_ref,
                      cin_ref, nin_ref, min_ref,
                      y_ref, cout_ref, nout_ref, mout_ref):
    r8 = SAMPLE_ROWS_PER_STEP
    li = sm_ref[:, S_IG:S_IG + M_HEADS] + ib_row_ref[...]
    lf = _log_sigmoid(sm_ref[:, S_FG:S_FG + M_HEADS] + fb_row_ref[...])
    m_prev = min_ref[...]
    inter = lf + m_prev
    s = jnp.maximum(inter, li)
    w_inter = jnp.exp(inter - s)
    w_in = jnp.exp(li - s)
    mout_ref[...] = s
    row_k = lax.broadcasted_iota(jnp.int32, (r8, M_DQK), 0)

    for h in range(M_HEADS):
        ksl = slice(h * M_DQK, (h + 1) * M_DQK)
        vsl = slice(h * M_DV, (h + 1) * M_DV)
        qh = q_ref[:, ksl]
        kh = k_ref[:, ksl] * (M_DQK ** -0.5)
        vh = v_ref[:, vsl]
        vh_bf = vh.astype(BF16)
        n_prev = nin_ref[:, ksl]
        wi = w_inter[:, h:h + 1]
        qk = jnp.sum(qh * kh, axis=1, keepdims=True) * w_in[:, h:h + 1]
        kw = kh * w_in[:, h:h + 1]
        qc = jnp.zeros((r8, M_DV), F32)
        for r in range(r8):
            c_prev = cin_ref[r, h]
            q_r = jnp.where(row_k == r, qh, 0.0).astype(BF16)
            qc = qc + _dot(q_r, c_prev.astype(BF16))
            kw_r = jnp.where(row_k == r, kw, 0.0).astype(BF16)
            cout_ref[r, h] = w_inter[r:r + 1, h:h + 1] * c_prev + _dot_lt(kw_r, vh_bf)
        nout_ref[:, ksl] = wi * n_prev + kw
        num = qk * vh + wi * qc
        den = qk + wi * jnp.sum(qh * n_prev, axis=1, keepdims=True)
        hm = num / jnp.maximum(jnp.abs(den), jnp.exp(-s[:, h:h + 1]))
        hn = hm * lax.rsqrt(jnp.mean(hm * hm, axis=-1, keepdims=True) + EPS) * mnorm_ref[:, vsl]
        y_ref[:, vsl] = hn * jax.nn.sigmoid(o_ref[:, vsl])


def _ml_sample(proj, row0, n_dec, layer, c_all, n_all, m_all, p, prev):
    r8 = SAMPLE_ROWS_PER_STEP
    rb0 = row0 // r8

    def pspec(width, name):
        return pl.BlockSpec((r8, width), lambda i: (rb0 + i, PROJ_SRC[name][1] // width))

    def wspec(shape):
        return pl.BlockSpec(shape, lambda i: (0,) * len(shape))

    cspec = pl.BlockSpec((None, r8, M_HEADS, M_DQK, M_DV), lambda i: (layer, i, 0, 0, 0))
    nspec = pl.BlockSpec((None, r8, M_HEADS * M_DQK), lambda i: (layer, i, 0))
    mspec = pl.BlockSpec((None, r8, M_HEADS), lambda i: (layer, i, 0))
    in_specs = [pspec(1024, "q"), pspec(1024, "k"), pspec(2048, "v"), pspec(2048, "o"), pspec(SM_W, "s"),
                wspec((1, 8)), wspec((1, 8)), wspec((1, 2048)), cspec, nspec, mspec]
    args = [*_proj_arrays(proj, "qkvos"), p["ib_row"], p["fb_row"], p["m_norm"], c_all, n_all, m_all]
    n_in = len(args)
    aliases = {}
    if prev is not None:
        in_specs += [pl.BlockSpec(memory_space=pl.ANY)] * 3
        args += list(prev)
        aliases = {n_in: 1, n_in + 1: 2, n_in + 2: 3}
    return pl.pallas_call(
        _drop_refs(_ml_sample_kernel, n_in, len(aliases)),
        grid=(n_dec // r8,),
        in_specs=in_specs,
        out_specs=[pl.BlockSpec((r8, D_MODEL), lambda i: (i, 0)), cspec, nspec, mspec],
        out_shape=[jax.ShapeDtypeStruct((n_dec, D_MODEL), F32),
                   jax.ShapeDtypeStruct(c_all.shape, F32),
                   jax.ShapeDtypeStruct(n_all.shape, F32),
                   jax.ShapeDtypeStruct(m_all.shape, F32)],
        input_output_aliases=aliases,
        compiler_params=_cp("parallel"),
        name="ml_sample",
    )(*args)


FFN_TN = 256


def _ffn_gate_prompt_kernel(a_ref, wg_ref, wu_ref, cw_ref, cb_ref, act_ref, tail_ref, gpad):
    seq = a_ref.shape[0]
    a = a_ref[...]
    g = _dot(a, wg_ref[...].astype(BF16))
    u = _dot(a, wu_ref[...].astype(BF16))
    gpad[0:8, :] = jnp.zeros((8, gpad.shape[1]), F32)
    gpad[8:seq + 8, :] = g
    acc = gpad[6:seq + 6, :] * cw_ref[0:1, :]
    acc = acc + gpad[7:seq + 7, :] * cw_ref[1:2, :]
    acc = acc + g * cw_ref[2:3, :]
    act_ref[...] = (_silu(acc + cb_ref[...]) * u).astype(act_ref.dtype)
    tail_ref[...] = g[seq - 8:seq, :]


def _ffn_gate_prompt(h2, w_gate, w_up, layer, n_batch, seq, n_rows_total, p):
    tn = FFN_TN
    k = h2.shape[1]
    wspec = pl.BlockSpec((None, k, tn), lambda b, j: (layer, 0, j))
    return pl.pallas_call(
        _ffn_gate_prompt_kernel,
        grid=(n_batch, D_FF // tn),
        in_specs=[_lhs_spec(seq, k), wspec, wspec,
                  pl.BlockSpec((3, tn), lambda b, j: (0, j)),
                  pl.BlockSpec((1, tn), lambda b, j: (0, j))],
        out_specs=[pl.BlockSpec((seq, tn), lambda b, j: (b, j)),
                   pl.BlockSpec((None, 8, tn), lambda b, j: (b, 0, j))],
        out_shape=[jax.ShapeDtypeStruct((n_rows_total, D_FF), BF16),
                   jax.ShapeDtypeStruct((n_batch, 8, D_FF), F32)],
        scratch_shapes=[pltpu.VMEM((seq + 8, tn), F32)],
        compiler_params=_cp("parallel", "arbitrary"),
        name="ffn_gate_prompt",
    )(h2, w_gate, w_up, p["ffn_cw"], p["ffn_cb"])


def _ffn_gate_sample_kernel(a_ref, wg_ref, wu_ref, s0_ref, s1_ref, cw_ref, cb_ref, act_any, act_ref, g_ref):
    del act_any
    a = a_ref[...]
    g = _dot(a, wg_ref[...].astype(BF16))
    u = _dot(a, wu_ref[...].astype(BF16))
    acc = s0_ref[...] * cw_ref[0:1, :]
    acc = acc + s1_ref[...] * cw_ref[1:2, :]
    acc = acc + g * cw_ref[2:3, :]
    act_ref[...] = (_silu(acc + cb_ref[...]) * u).astype(act_ref.dtype)
    g_ref[...] = g


def _ffn_gate_sample(h2, w_gate, w_up, layer, row0, n_dec, conv_state_all, p, act):
    tn = FFN_TN
    k = h2.shape[1]
    rb = row0 // n_dec
    nct = D_FF // tn
    wspec = pl.BlockSpec((None, k, tn), lambda j: (layer, 0, j))
    return pl.pallas_call(
        _ffn_gate_sample_kernel,
        grid=(nct,),
        in_specs=[pl.BlockSpec((n_dec, k), lambda j: (rb, 0)), wspec, wspec,
                  pl.BlockSpec((None, n_dec, tn), lambda j: (layer, 0, j)),
                  pl.BlockSpec((None, n_dec, tn), lambda j: (layer, 0, nct + j)),
                  pl.BlockSpec((3, tn), lambda j: (0, j)),
                  pl.BlockSpec((1, tn), lambda j: (0, j)),
                  pl.BlockSpec(memory_space=pl.ANY)],
        out_specs=[pl.BlockSpec((n_dec, tn), lambda j: (rb, j)),
                   pl.BlockSpec((n_dec, tn), lambda j: (0, j))],
        out_shape=[jax.ShapeDtypeStruct(act.shape, act.dtype),
                   jax.ShapeDtypeStruct((n_dec, D_FF), F32)],
        input_output_aliases={7: 0},
        compiler_params=_cp("arbitrary"),
        name="ffn_gate_sample",
    )(h2, w_gate, w_up, conv_state_all, conv_state_all, p["ffn_cw"], p["ffn_cb"], act)


def _pick_tile(m, candidates):
    for t in candidates:
        if m % t == 0:
            return t
    raise ValueError(f"no row tile for {m}")


def _layer_params(l, ssd_conv_w, ssd_conv_b, ssd_dt_bias, ssd_a_log, ssd_d, ssd_norm, gm_norm, gm_ws, gm_bs,
                  m_i_bias, m_f_bias, m_norm, ffn_conv_w, ffn_conv_b):
    cw, cb = ssd_conv_w[l], ssd_conv_b[l]
    return dict(
        cwx=cw[:, 0:2048], cbx=cb[None, 0:2048], cwb=cw[:, 2048:2560], cbb=cb[None, 2048:2560],
        cwc=cw[:, 2560:3072], cbc=cb[None, 2560:3072],
        dtb_row=ssd_dt_bias[l][None, :], dtb_col=ssd_dt_bias[l][:, None],
        alog_row=ssd_a_log[l][None, :], alog_col=ssd_a_log[l][:, None],
        dfull=jnp.repeat(ssd_d[l], SSD_HEAD_DIM)[None, :], ssd_norm=ssd_norm[l][None, :],
        gm_norm=gm_norm[l][None, :], gm_ws=gm_ws[l], gm_bst=gm_bs[l].T,
        gm_w0=jnp.repeat(gm_ws[l][:, 0, 0], GM_GROUP_DIM)[None, :],
        gm_b0=jnp.repeat(gm_bs[l][:, 0], GM_GROUP_DIM)[None, :],
        ib_row=m_i_bias[l][None, :], ib_col=m_i_bias[l][:, None],
        fb_row=m_f_bias[l][None, :], fb_col=m_f_bias[l][:, None], m_norm=m_norm[l][None, :],
        ffn_cw=ffn_conv_w[l], ffn_cb=ffn_conv_b[l][None, :],
    )


def kernel(x_prompt, x_sample, state_ssd, state_ssd_conv, state_mlstm_c, state_mlstm_n, state_mlstm_m,
           state_ffn_conv, norm1, w_in, ssd_conv_w, ssd_conv_b, ssd_dt_bias, ssd_a_log, ssd_d, ssd_norm,
           gm_norm, gm_ws, gm_bs, m_i_bias, m_f_bias, m_norm, w_branch, w_out, norm2,
           w_gate, w_up, ffn_conv_w, ffn_conv_b, w_down, final_norm):
    n_batch, seq, d = x_prompt.shape
    n_dec = x_sample.shape[0]
    depth = w_in.shape[0]
    n_chunks = seq // CHUNK
    n_prompt = n_batch * seq
    n_rows = n_prompt + n_dec
    tm = _pick_tile(n_rows, (832, 640, 384, 128))
    tm_mid = _pick_tile(n_rows, (2080, 1664, 832, 640, 384, 128))
    tm_big = _pick_tile(n_rows, (4160, 2080, 1664, 832, 640, 384, 128))
    w_in_t = jnp.swapaxes(w_in, 1, 2)

    ssd_conv_all = state_ssd_conv.reshape(depth, n_dec, -1)
    ssd_h_all = state_ssd.reshape(depth, n_dec, D_MODEL, SSD_STATE)
    ml_n_all = state_mlstm_n.reshape(depth, n_dec, -1)
    ffn_conv_all = state_ffn_conv.reshape(depth, n_dec, -1)
    s_ssd = None
    s_ml = None

    def xbc(rows):
        c0 = PROJ_SRC["x"][1]
        return rows[..., c0:c0 + 3072]

    def tail_rows(a, n_tail):
        return jnp.stack([a[(b + 1) * seq - n_tail:(b + 1) * seq] for b in range(n_batch)])

    x = jnp.concatenate([x_prompt.reshape(n_prompt, d), x_sample.reshape(n_dec, d)], axis=0)
    outs = [[] for _ in range(13)]
    for l in range(depth):
        p = _layer_params(l, ssd_conv_w, ssd_conv_b, ssd_dt_bias, ssd_a_log, ssd_d, ssd_norm, gm_norm, gm_ws, gm_bs,
                          m_i_bias, m_f_bias, m_norm, ffn_conv_w, ffn_conv_b)
        h = _rmsnorm_rows(x, norm1[l], BF16, tm)
        proj = {"a": _matmul_t(h, w_in_t, l, tm_big, 256, *W_IN_A),
                "b": _matmul_t(h, w_in_t, l, tm_big, 256, *W_IN_B),
                "g": _matmul_t(h, w_in_t, l, tm_big, 256, *W_IN_G),
                "s": _matmul_small(h, w_in_t, l, tm_mid)}

        y0, p_ssd = _ssd_prompt(proj, n_batch, n_chunks, n_rows, p)
        y1 = _gm_prompt(proj, n_prompt, n_rows, p)
        y2, p_c, p_n, p_m = _ml_prompt(proj, n_batch, n_chunks, n_rows, p)

        ys0, s_ssd = _ssd_sample(proj, n_prompt, n_dec, l, ssd_conv_all, ssd_h_all, p, s_ssd)
        ys2, *s_ml = _ml_sample(proj, n_prompt, n_dec, l, state_mlstm_c, ml_n_all, state_mlstm_m, p, s_ml)
        y0, y1, y2, s_vn = _gm_sample_pack(proj, n_prompt, n_dec, p, ys0, ys2, y0, y1, y2)

        merged = _merge(y0, y1, y2, w_branch, l, proj, tm, 256)
        x = _matmul(merged, w_out, l, tm_mid, 512, residual=x)

        h2 = _rmsnorm_rows(x, norm2[l], BF16, tm)
        act, g_tail = _ffn_gate_prompt(h2, w_gate, w_up, l, n_batch, seq, n_rows, p)
        act, g_dec = _ffn_gate_sample(h2, w_gate, w_up, l, n_prompt, n_dec, ffn_conv_all, p, act)
        x = _matmul(act, w_down, l, tm, 512, residual=x)

        outs[0].append(p_ssd.reshape(n_batch, SSD_HEADS, SSD_HEAD_DIM, SSD_STATE))
        outs[1].append(xbc(tail_rows(proj["a"], 3)))
        outs[2].append(p_c)
        outs[3].append(p_n)
        outs[4].append(p_m[:, :, 0])
        outs[5].append(g_tail[:, 6:8, :])
        outs[7].append(jnp.concatenate([state_ssd_conv[l][:, 1:], xbc(proj["a"][n_prompt:])[:, None, :]], axis=1))
        outs[11].append(jnp.concatenate([state_ffn_conv[l][:, 1:], g_dec[:, None, :]], axis=1))
        outs[12].append(s_vn.reshape(n_dec, 1, D_MODEL))

    tf = _pick_tile(n_prompt, (1024, 512, 256, 128))
    y_prompt = _rmsnorm_rows(x, final_norm, F32, tf, 0, n_prompt).reshape(n_batch, seq, d)
    y_sample = _rmsnorm_rows(x, final_norm, F32, n_dec, n_prompt // n_dec, n_dec).reshape(n_dec, 1, d)
    s_c, s_n, s_m = s_ml
    stacked = {6: s_ssd.reshape(depth, n_dec, SSD_HEADS, SSD_HEAD_DIM, SSD_STATE), 8: s_c,
               9: s_n.reshape(depth, n_dec, M_HEADS, M_DQK), 10: s_m}
    return (y_prompt, y_sample) + tuple(stacked[i] if i in stacked else jnp.stack(outs[i]) for i in range(13))
```

```python
import functools

import jax
import jax.numpy as jnp
from jax import lax
from jax.experimental import pallas as pl
from jax.experimental.pallas import tpu as pltpu

F32 = jnp.float32
BF16 = jnp.bfloat16

D_MODEL = 2048
DEPTH = 4
CHUNK = 128
EPS = 1e-6
SSD_HEADS = 32
SSD_HEAD_DIM = 64
SSD_GROUPS = 4
SSD_STATE = 128
SSD_GROUP_COLS = D_MODEL // SSD_GROUPS
GM_GROUPS = 8
GM_GROUP_DIM = 256
M_HEADS = 8
M_DV = 256
M_DQK = 128
D_FF = 5632

PROJ_SRC = {"z": ("a", 0), "x": ("a", 2048), "B": ("a", 4096), "C": ("a", 4608),
            "u": ("b", 0), "vg": ("b", 2048), "q": ("b", 4096), "k": ("b", 5120), "v": ("b", 6144), "o": ("b", 8192),
            "g": ("g", 0), "s": ("s", 0)}
W_IN_A = (0, 5120)
W_IN_B = (5152, 10240)
W_IN_G = (15408, 6144)
W_IN_DT_COL = 5120
W_IN_IF_COL = 15392
SM_W = 256
S_DT = 0
S_IG = 128
S_FG = 136

VMEM_LIMIT_BYTES = 56 * 1024 * 1024
SAMPLE_ROWS_PER_STEP = 8


def _cp(*sem):
    return pltpu.CompilerParams(dimension_semantics=sem, vmem_limit_bytes=VMEM_LIMIT_BYTES)


def _dot(a, b):
    return jnp.dot(a, b, preferred_element_type=F32)


def _dot_rt(a, b):
    return lax.dot_general(a, b, (((1,), (1,)), ((), ())), preferred_element_type=F32)


def _dot_lt(a, b):
    return lax.dot_general(a, b, (((0,), (0,)), ((), ())), preferred_element_type=F32)


def _split3(a):
    a1 = a.astype(BF16)
    r1 = a - a1.astype(F32)
    a2 = r1.astype(BF16)
    a3 = (r1 - a2.astype(F32)).astype(BF16)
    return a1, a2, a3


def _dot3_l(a_f32, b_bf16):
    a1, a2, a3 = _split3(a_f32)
    return (_dot(a1, b_bf16) + _dot(a2, b_bf16)) + _dot(a3, b_bf16)


def _dot3_r(a_bf16, b_f32):
    b1, b2, b3 = _split3(b_f32)
    return (_dot(a_bf16, b1) + _dot(a_bf16, b2)) + _dot(a_bf16, b3)


def _tri(n, lower=True):
    r = lax.broadcasted_iota(jnp.int32, (n, n), 0)
    c = lax.broadcasted_iota(jnp.int32, (n, n), 1)
    return (r >= c) if lower else (r <= c)


def _softplus(x):
    return jnp.maximum(x, 0.0) + jnp.log1p(jnp.exp(-jnp.abs(x)))


def _log_sigmoid(x):
    return -_softplus(-x)


def _silu(x):
    return x * jax.nn.sigmoid(x)


def _rms_kernel(x_ref, g_ref, o_ref):
    x = x_ref[...]
    y = x * lax.rsqrt(jnp.mean(x * x, axis=-1, keepdims=True) + EPS)
    o_ref[...] = (y * g_ref[...]).astype(o_ref.dtype)


def _rmsnorm_rows(x, g, out_dtype, tm, row_block0=0, n_rows=None):
    m_total, d = x.shape
    n_rows = m_total if n_rows is None else n_rows
    return pl.pallas_call(
        _rms_kernel,
        grid=(n_rows // tm,),
        in_specs=[pl.BlockSpec((tm, d), lambda i: (row_block0 + i, 0)),
                  pl.BlockSpec((1, d), lambda i: (0, 0))],
        out_specs=pl.BlockSpec((tm, d), lambda i: (i, 0)),
        out_shape=jax.ShapeDtypeStruct((n_rows, d), out_dtype),
        compiler_params=_cp("parallel"),
        name="rmsnorm_rows",
    )(x, g.reshape(1, d))


def _mm_kernel(has_res, *refs):
    a_ref, w_ref = refs[0], refs[1]
    acc = _dot(a_ref[...], w_ref[...].astype(BF16))
    if has_res:
        acc = refs[2][...] + acc
    o_ref = refs[-1]
    o_ref[...] = acc.astype(o_ref.dtype)


SINGLE_BUFFER_LHS_BYTES = 6 * 1024 * 1024


def _lhs_spec(tm, k):
    if tm * k * 2 >= SINGLE_BUFFER_LHS_BYTES:
        return pl.BlockSpec((tm, k), lambda i, j: (i, 0), pipeline_mode=pl.Buffered(1))
    return pl.BlockSpec((tm, k), lambda i, j: (i, 0))


def _matmul(a, w, layer, tm, tn, residual=None):
    m, k = a.shape
    n = w.shape[-1]
    in_specs = [_lhs_spec(tm, k),
                pl.BlockSpec((None, k, tn), lambda i, j: (layer, 0, j))]
    args = [a, w]
    if residual is not None:
        in_specs.append(pl.BlockSpec((tm, tn), lambda i, j: (i, j)))
        args.append(residual)
    return pl.pallas_call(
        functools.partial(_mm_kernel, residual is not None),
        grid=(m // tm, n // tn),
        in_specs=in_specs,
        out_specs=pl.BlockSpec((tm, tn), lambda i, j: (i, j)),
        out_shape=jax.ShapeDtypeStruct((m, n), F32),
        compiler_params=_cp("parallel", "arbitrary"),
        name="matmul_res" if residual is not None else "matmul",
    )(*args)


def _mm_t_kernel(a_ref, wt_ref, o_ref):
    o_ref[...] = _dot_rt(a_ref[...], wt_ref[0].astype(BF16))


def _wt_spec(layer, rows, k, row_of_step):
    def index_map(*idx):
        r = row_of_step(*idx)
        return (layer, r if isinstance(r, int) else pl.multiple_of(r, 8), 0)

    return pl.BlockSpec((pl.Element(1), pl.Element(rows), pl.Element(k)), index_map)


def _matmul_t(a, wt, layer, tm, tn, col0, n):
    m, k = a.shape
    return pl.pallas_call(
        _mm_t_kernel,
        grid=(m // tm, n // tn),
        in_specs=[_lhs_spec(tm, k),
                  _wt_spec(layer, tn, k, lambda i, j: col0 + tn * j)],
        out_specs=pl.BlockSpec((tm, tn), lambda i, j: (i, j)),
        out_shape=jax.ShapeDtypeStruct((m, n), F32),
        compiler_params=_cp("parallel", "arbitrary"),
        name="matmul_t",
    )(a, wt)


def _mm_small_kernel(a_ref, wdt_ref, wif_ref, o_ref):
    a = a_ref[...]
    o_ref[...] = jnp.zeros(o_ref.shape, F32)
    o_ref[:, S_DT:S_DT + SSD_HEADS] = _dot_rt(a, wdt_ref[0].astype(BF16))
    o_ref[:, S_IG:S_IG + 2 * M_HEADS] = _dot_rt(a, wif_ref[0].astype(BF16))


def _matmul_small(a, wt, layer, tm):
    m, k = a.shape
    return pl.pallas_call(
        _mm_small_kernel,
        grid=(m // tm,),
        in_specs=[pl.BlockSpec((tm, k), lambda i: (i, 0)),
                  _wt_spec(layer, SSD_HEADS, k, lambda i: W_IN_DT_COL),
                  _wt_spec(layer, 2 * M_HEADS, k, lambda i: W_IN_IF_COL)],
        out_specs=pl.BlockSpec((tm, SM_W), lambda i: (i, 0)),
        out_shape=jax.ShapeDtypeStruct((m, SM_W), F32),
        compiler_params=_cp("parallel"),
        name="matmul_small",
    )(a, wt, wt)


def _merge_kernel(y0_ref, y1_ref, y2_ref, w_ref, g0_ref, g1_ref, g2_ref, o_ref):
    acc = jax.nn.sigmoid(g0_ref[...]) * _dot(y0_ref[...], w_ref[0].astype(BF16))
    acc = acc + jax.nn.sigmoid(g1_ref[...]) * _dot(y1_ref[...], w_ref[1].astype(BF16))
    acc = acc + jax.nn.sigmoid(g2_ref[...]) * _dot(y2_ref[...], w_ref[2].astype(BF16))
    o_ref[...] = acc.astype(o_ref.dtype)


def _merge(y0, y1, y2, w_branch, layer, proj, tm, tn):
    m = y0.shape[0]
    gb = 0
    gstep = D_MODEL // tn
    yspec = _lhs_spec(tm, D_MODEL)

    def gspec(b):
        return pl.BlockSpec((tm, tn), lambda i, j: (i, gb + b * gstep + j))

    return pl.pallas_call(
        _merge_kernel,
        grid=(m // tm, D_MODEL // tn),
        in_specs=[yspec, yspec, yspec,
                  pl.BlockSpec((None, 3, D_MODEL, tn), lambda i, j: (layer, 0, 0, j)),
                  gspec(0), gspec(1), gspec(2)],
        out_specs=pl.BlockSpec((tm, tn), lambda i, j: (i, j)),
        out_shape=jax.ShapeDtypeStruct((m, D_MODEL), BF16),
        compiler_params=_cp("parallel", "arbitrary"),
        name="merge",
    )(y0, y1, y2, w_branch, proj["g"], proj["g"], proj["g"])


def _pair_select(lane_lo, col_a, col_b):
    return jnp.where(lane_lo, col_a, col_b)


def _ssd_prompt_kernel(z_ref, x_ref, b_ref, c_ref, sm_ref,
                       cwx_ref, cbx_ref, cwb_ref, cbb_ref, cwc_ref, cbc_ref,
                       dtb_row_ref, dtb_col_ref, alog_row_ref, alog_col_ref, dfull_ref, norm_ref,
                       y_ref, hout_ref,
                       xpad, bpad, cpad, ht_scr, ybuf):
    c = pl.program_id(1)
    nc = pl.num_programs(1)
    q = CHUNK

    @pl.when(c == 0)
    def _():
        xpad[0:8, :] = jnp.zeros((8, xpad.shape[1]), F32)
        bpad[0:8, :] = jnp.zeros((8, bpad.shape[1]), F32)
        cpad[0:8, :] = jnp.zeros((8, cpad.shape[1]), F32)
        ht_scr[...] = jnp.zeros(ht_scr.shape, F32)

    @pl.when(c > 0)
    def _():
        xpad[0:8, :] = xpad[q:q + 8, :]
        bpad[0:8, :] = bpad[q:q + 8, :]
        cpad[0:8, :] = cpad[q:q + 8, :]

    xpad[8:q + 8, :] = x_ref[...]
    bpad[8:q + 8, :] = b_ref[...]
    cpad[8:q + 8, :] = c_ref[...]

    def conv(pad, w_ref, bias_ref, lanes):
        acc = pad[5:q + 5, lanes] * w_ref[0:1, lanes]
        acc = acc + pad[6:q + 6, lanes] * w_ref[1:2, lanes]
        acc = acc + pad[7:q + 7, lanes] * w_ref[2:3, lanes]
        acc = acc + pad[8:q + 8, lanes] * w_ref[3:4, lanes]
        return _silu(acc + bias_ref[:, lanes])

    bm = conv(bpad, cwb_ref, cbb_ref, slice(None))
    cm = conv(cpad, cwc_ref, cbc_ref, slice(None))

    sm = sm_ref[...]
    sm_t = sm.T
    dt_col = _softplus(sm[:, S_DT:S_DT + SSD_HEADS] + dtb_row_ref[...])
    dt_row = _softplus(sm_t[S_DT:S_DT + SSD_HEADS, :] + dtb_col_ref[...])
    da_col = dt_col * (-jnp.exp(alog_row_ref[...]))
    da_row = dt_row * (-jnp.exp(alog_col_ref[...]))
    tril = _tri(q, True)
    s_col = _dot3_r(tril.astype(BF16), da_col)
    s_row = _dot3_l(da_row, _tri(q, False).astype(BF16))
    es_col = jnp.exp(s_col)
    s_last = s_col[q - 1:q, :]
    wend_col = jnp.exp(s_last - s_col) * dt_col
    elast = jnp.exp(s_last)

    lane_lo = lax.broadcasted_iota(jnp.int32, (q, 128), 1) < SSD_HEAD_DIM
    lane_lo1 = lax.broadcasted_iota(jnp.int32, (1, 128), 1) < SSD_HEAD_DIM

    for g in range(SSD_GROUPS):
        bg = bm[:, g * SSD_STATE:(g + 1) * SSD_STATE]
        cg = cm[:, g * SSD_STATE:(g + 1) * SSD_STATE].astype(BF16)
        bg_t = bg.T.astype(BF16)
        cb = _dot(cg, bg_t)
        gsl = slice(g * SSD_GROUP_COLS, (g + 1) * SSD_GROUP_COLS)
        yint = _dot(cg, ht_scr[:, gsl].astype(BF16))
        for jj in range(4):
            j = g * 4 + jj
            ha, hb = 2 * j, 2 * j + 1
            psl = slice(j * 128, (j + 1) * 128)
            xp = conv(xpad, cwx_ref, cbx_ref, psl)
            xp_bf = xp.astype(BF16)
            ys = []
            for h in (ha, hb):
                dec = jnp.exp(jnp.where(tril, s_col[:, h:h + 1] - s_row[h:h + 1, :], -jnp.inf))
                w = cb * dec * dt_row[h:h + 1, :]
                ys.append(_dot(w.astype(BF16), xp_bf))
            y = jnp.where(lane_lo, ys[0], ys[1])
            y = y + yint[:, jj * 128:(jj + 1) * 128] * _pair_select(lane_lo, es_col[:, ha:ha + 1], es_col[:, hb:hb + 1])
            y = y + dfull_ref[:, psl] * xp
            ybuf[:, psl] = y * _silu(z_ref[:, psl])
            xw = xp * _pair_select(lane_lo, wend_col[:, ha:ha + 1], wend_col[:, hb:hb + 1])
            dpair = _pair_select(lane_lo1, elast[:, ha:ha + 1], elast[:, hb:hb + 1])
            ht_scr[:, psl] = ht_scr[:, psl] * dpair + _dot(bg_t, xw.astype(BF16))
        yg = ybuf[:, gsl]
        yn = yg * lax.rsqrt(jnp.mean(yg * yg, axis=-1, keepdims=True) + EPS)
        y_ref[:, gsl] = (yn * norm_ref[:, gsl]).astype(y_ref.dtype)

    @pl.when(c == nc - 1)
    def _():
        hout_ref[...] = ht_scr[...].T


def _proj_arrays(proj, names):
    return [proj[PROJ_SRC[n][0]] for n in names]


def _ssd_prompt(proj, n_batch, n_chunks, n_rows_total, p):
    q = CHUNK
    rb = lambda b, c: b * n_chunks + c

    def pspec(width, name):
        return pl.BlockSpec((q, width), lambda b, c: (rb(b, c), PROJ_SRC[name][1] // width))

    def wspec(shape):
        return pl.BlockSpec(shape, lambda b, c: (0,) * len(shape))

    return pl.pallas_call(
        _ssd_prompt_kernel,
        grid=(n_batch, n_chunks),
        in_specs=[pspec(2048, "z"), pspec(2048, "x"), pspec(512, "B"), pspec(512, "C"), pspec(SM_W, "s"),
                  wspec((4, 2048)), wspec((1, 2048)), wspec((4, 512)), wspec((1, 512)), wspec((4, 512)), wspec((1, 512)),
                  wspec((1, 32)), wspec((32, 1)), wspec((1, 32)), wspec((32, 1)), wspec((1, 2048)), wspec((1, 2048))],
        out_specs=[pl.BlockSpec((q, D_MODEL), lambda b, c: (rb(b, c), 0)),
                   pl.BlockSpec((None, D_MODEL, SSD_STATE), lambda b, c: (b, 0, 0))],
        out_shape=[jax.ShapeDtypeStruct((n_rows_total, D_MODEL), BF16),
                   jax.ShapeDtypeStruct((n_batch, D_MODEL, SSD_STATE), F32)],
        scratch_shapes=[pltpu.VMEM((q + 8, 2048), F32), pltpu.VMEM((q + 8, 512), F32), pltpu.VMEM((q + 8, 512), F32),
                        pltpu.VMEM((SSD_STATE, D_MODEL), F32), pltpu.VMEM((q, D_MODEL), F32)],
        compiler_params=_cp("parallel", "arbitrary"),
        name="ssd_prompt",
    )(*_proj_arrays(proj, "zxBCs"),
      p["cwx"], p["cbx"], p["cwb"], p["cbb"], p["cwc"], p["cbc"],
      p["dtb_row"], p["dtb_col"], p["alog_row"], p["alog_col"], p["dfull"], p["ssd_norm"])


def _ssd_sample_kernel(z_ref, x_ref, b_ref, c_ref, sm_ref, cs0_ref, cs1_ref, cs2_ref,
                       cwx_ref, cbx_ref, cwb_ref, cbb_ref, cwc_ref, cbc_ref,
                       dtb_row_ref, alog_row_ref, dfull_ref, norm_ref, hin_ref,
                       y_ref, hout_ref):
    r8 = SAMPLE_ROWS_PER_STEP

    def conv(lo, hi, new, w_ref, bias_ref):
        acc = cs0_ref[:, lo:hi] * w_ref[0:1, :]
        acc = acc + cs1_ref[:, lo:hi] * w_ref[1:2, :]
        acc = acc + cs2_ref[:, lo:hi] * w_ref[2:3, :]
        acc = acc + new * w_ref[3:4, :]
        return _silu(acc + bias_ref[...])

    xs = conv(0, 2048, x_ref[...], cwx_ref, cbx_ref)
    bm = conv(2048, 2560, b_ref[...], cwb_ref, cbb_ref)
    cm = conv(2560, 3072, c_ref[...], cwc_ref, cbc_ref)
    dt = _softplus(sm_ref[:, S_DT:S_DT + SSD_HEADS] + dtb_row_ref[...])
    e = jnp.exp(dt * (-jnp.exp(alog_row_ref[...])))
    hh = lax.broadcasted_iota(jnp.int32, (SSD_HEADS, D_MODEL), 0)
    cc = lax.broadcasted_iota(jnp.int32, (SSD_HEADS, D_MODEL), 1)
    expand = jnp.where((cc >= hh * SSD_HEAD_DIM) & (cc < (hh + 1) * SSD_HEAD_DIM), 1.0, 0.0).astype(BF16)
    dt_full = _dot3_l(dt, expand)
    e_full = _dot3_l(e, expand)
    dx = xs * dt_full
    e1, e2, e3 = (t.astype(F32) for t in _split3(e_full))
    row = lax.broadcasted_iota(jnp.int32, (r8, SSD_GROUP_COLS), 0)
    row_n = lax.broadcasted_iota(jnp.int32, (r8, SSD_STATE), 0)
    ones = jnp.ones((r8, SSD_STATE), BF16)

    ygroups = []
    for g in range(SSD_GROUPS):
        gsl = slice(g * SSD_GROUP_COLS, (g + 1) * SSD_GROUP_COLS)
        nsl = slice(g * SSD_STATE, (g + 1) * SSD_STATE)
        bg = bm[:, nsl].astype(BF16)
        cg = cm[:, nsl]
        dxg = dx[:, gsl]
        yacc = jnp.zeros((r8, SSD_GROUP_COLS), F32)
        for r in range(r8):
            a_e = jnp.where(row == 0, e1[r:r + 1, gsl],
                            jnp.where(row == 1, e2[r:r + 1, gsl],
                                      jnp.where(row == 2, e3[r:r + 1, gsl], 0.0))).astype(BF16)
            ecol = _dot_lt(a_e, ones)
            a_x = jnp.where(row == r, dxg, 0.0).astype(BF16)
            hn = hin_ref[r, gsl, :] * ecol + _dot_lt(a_x, bg)
            hout_ref[r, gsl, :] = hn
            c_r = jnp.where(row_n == r, cg, 0.0).astype(BF16)
            yacc = yacc + _dot_rt(c_r, hn.astype(BF16))
        ygroups.append(yacc)

    for g in range(SSD_GROUPS):
        gsl = slice(g * SSD_GROUP_COLS, (g + 1) * SSD_GROUP_COLS)
        y = ygroups[g] + dfull_ref[:, gsl] * xs[:, gsl]
        y = y * _silu(z_ref[:, gsl])
        yn = y * lax.rsqrt(jnp.mean(y * y, axis=-1, keepdims=True) + EPS)
        y_ref[:, gsl] = yn * norm_ref[:, gsl]


def _drop_refs(kernel_fn, n_inputs, n_dropped):
    def body(*refs):
        return kernel_fn(*refs[:n_inputs], *refs[n_inputs + n_dropped:])
    return body


def _ssd_sample(proj, row0, n_dec, layer, conv_state_all, h_state_all, p, h_out_prev):
    r8 = SAMPLE_ROWS_PER_STEP
    rb0 = row0 // r8
    depth = h_state_all.shape[0]

    def pspec(width, name):
        return pl.BlockSpec((r8, width), lambda i: (rb0 + i, PROJ_SRC[name][1] // width))

    def wspec(shape):
        return pl.BlockSpec(shape, lambda i: (0,) * len(shape))

    def cspec(j):
        return pl.BlockSpec((None, r8, 3072), lambda i: (layer, i, j))

    hspec = pl.BlockSpec((None, r8, D_MODEL, SSD_STATE), lambda i: (layer, i, 0, 0))
    in_specs = [pspec(2048, "z"), pspec(2048, "x"), pspec(512, "B"), pspec(512, "C"), pspec(SM_W, "s"),
                cspec(0), cspec(1), cspec(2),
                wspec((4, 2048)), wspec((1, 2048)), wspec((4, 512)), wspec((1, 512)), wspec((4, 512)), wspec((1, 512)),
                wspec((1, 32)), wspec((1, 32)), wspec((1, 2048)), wspec((1, 2048)), hspec]
    args = [*_proj_arrays(proj, "zxBCs"), conv_state_all, conv_state_all, conv_state_all,
            p["cwx"], p["cbx"], p["cwb"], p["cbb"], p["cwc"], p["cbc"],
            p["dtb_row"], p["alog_row"], p["dfull"], p["ssd_norm"], h_state_all]
    n_in = len(args)
    aliases = {}
    if h_out_prev is not None:
        in_specs.append(pl.BlockSpec(memory_space=pl.ANY))
        args.append(h_out_prev)
        aliases = {n_in: 1}
    return pl.pallas_call(
        _drop_refs(_ssd_sample_kernel, n_in, len(aliases)),
        grid=(n_dec // r8,),
        in_specs=in_specs,
        out_specs=[pl.BlockSpec((r8, D_MODEL), lambda i: (i, 0)), hspec],
        out_shape=[jax.ShapeDtypeStruct((n_dec, D_MODEL), F32),
                   jax.ShapeDtypeStruct((depth, n_dec, D_MODEL, SSD_STATE), F32)],
        input_output_aliases=aliases,
        compiler_params=_cp("parallel"),
        name="ssd_sample",
    )(*args)


GM_ROWS_PER_STEP = 256


def _gm_prompt_kernel(u_ref, v_ref, gnorm_ref, ws_ref, bst_ref, y_ref):
    q = CHUNK
    tril = _tri(q, True)
    for cc in range(GM_ROWS_PER_STEP // q):
        rsl = slice(cc * q, (cc + 1) * q)
        u = jax.nn.gelu(u_ref[rsl, :])
        v = jax.nn.gelu(v_ref[rsl, :])
        vn = v * lax.rsqrt(jnp.mean(v * v, axis=-1, keepdims=True) + EPS) * gnorm_ref[...]
        for g in range(GM_GROUPS):
            gsl = slice(g * GM_GROUP_DIM, (g + 1) * GM_GROUP_DIM)
            w = jnp.where(tril, ws_ref[g], 0.0).astype(BF16)
            mixed = _dot(w, vn[:, gsl].astype(BF16)) + bst_ref[:, g:g + 1]
            y_ref[rsl, gsl] = (u[:, gsl] * mixed).astype(y_ref.dtype)


def _gm_prompt(proj, n_prompt_rows, n_rows_total, p):
    r = GM_ROWS_PER_STEP
    return pl.pallas_call(
        _gm_prompt_kernel,
        grid=(n_prompt_rows // r,),
        in_specs=[pl.BlockSpec((r, 2048), lambda i: (i, PROJ_SRC["u"][1] // 2048)),
                  pl.BlockSpec((r, 2048), lambda i: (i, PROJ_SRC["vg"][1] // 2048)),
                  pl.BlockSpec((1, 2048), lambda i: (0, 0)),
                  pl.BlockSpec((GM_GROUPS, CHUNK, CHUNK), lambda i: (0, 0, 0)),
                  pl.BlockSpec((CHUNK, GM_GROUPS), lambda i: (0, 0))],
        out_specs=pl.BlockSpec((r, D_MODEL), lambda i: (i, 0)),
        out_shape=jax.ShapeDtypeStruct((n_rows_total, D_MODEL), BF16),
        compiler_params=_cp("parallel"),
        name="gm_prompt",
    )(proj["b"], proj["b"], p["gm_norm"], p["gm_ws"], p["gm_bst"])


def _gm_sample_pack_kernel(u_ref, v_ref, gnorm_ref, w0_ref, b0_ref, yssd_ref, ym_ref,
                           y0_any, y1_any, y2_any,
                           o0_ref, o1_ref, o2_ref, vn_ref):
    del y0_any, y1_any, y2_any
    u = jax.nn.gelu(u_ref[...])
    v = jax.nn.gelu(v_ref[...])
    vn = v * lax.rsqrt(jnp.mean(v * v, axis=-1, keepdims=True) + EPS) * gnorm_ref[...]
    vn_ref[...] = vn
    mixed = w0_ref[...] * vn + b0_ref[...]
    o0_ref[...] = yssd_ref[...].astype(o0_ref.dtype)
    o1_ref[...] = (u * mixed).astype(o1_ref.dtype)
    o2_ref[...] = ym_ref[...].astype(o2_ref.dtype)


def _gm_sample_pack(proj, row0, n_dec, p, yssd_s, ym_s, y0, y1, y2):
    rb = row0 // n_dec
    full = lambda shape: pl.BlockSpec(shape, lambda i: (0,) * len(shape))
    anyspec = pl.BlockSpec(memory_space=pl.ANY)
    ospec = pl.BlockSpec((n_dec, D_MODEL), lambda i: (rb, 0))
    return pl.pallas_call(
        _gm_sample_pack_kernel,
        grid=(1,),
        in_specs=[pl.BlockSpec((n_dec, 2048), lambda i: (rb, PROJ_SRC["u"][1] // 2048)),
                  pl.BlockSpec((n_dec, 2048), lambda i: (rb, PROJ_SRC["vg"][1] // 2048)),
                  full((1, 2048)), full((1, 2048)), full((1, 2048)),
                  full((n_dec, D_MODEL)), full((n_dec, D_MODEL)),
                  anyspec, anyspec, anyspec],
        out_specs=[ospec, ospec, ospec, full((n_dec, D_MODEL))],
        out_shape=[jax.ShapeDtypeStruct(y0.shape, y0.dtype), jax.ShapeDtypeStruct(y1.shape, y1.dtype),
                   jax.ShapeDtypeStruct(y2.shape, y2.dtype), jax.ShapeDtypeStruct((n_dec, D_MODEL), F32)],
        input_output_aliases={7: 0, 8: 1, 9: 2},
        compiler_params=_cp("arbitrary"),
        name="gm_sample_pack",
    )(proj["b"], proj["b"], p["gm_norm"], p["gm_w0"], p["gm_b0"], yssd_s, ym_s, y0, y1, y2)


def _ml_prompt_kernel(q_ref, k_ref, v_ref, o_ref, sm_ref,
                      ib_row_ref, ib_col_ref, fb_row_ref, fb_col_ref, mnorm_ref,
                      y_ref, c_ref, n_ref, m_ref):
    c = pl.program_id(1)
    q = CHUNK

    @pl.when(c == 0)
    def _():
        c_ref[...] = jnp.zeros(c_ref.shape, F32)
        n_ref[...] = jnp.zeros(n_ref.shape, F32)
        m_ref[...] = jnp.zeros(m_ref.shape, F32)

    sm = sm_ref[...]
    sm_t = sm.T
    li_col = sm[:, S_IG:S_IG + M_HEADS] + ib_row_ref[...]
    lf_col = _log_sigmoid(sm[:, S_FG:S_FG + M_HEADS] + fb_row_ref[...])
    li_row = sm_t[S_IG:S_IG + M_HEADS, :] + ib_col_ref[...]
    lf_row = _log_sigmoid(sm_t[S_FG:S_FG + M_HEADS, :] + fb_col_ref[...])
    tril = _tri(q, True)
    b_col = _dot3_r(tril.astype(BF16), lf_col)
    b_row = _dot3_l(lf_row, _tri(q, False).astype(BF16))
    b_last = b_col[q - 1:q, :]

    for h in range(M_HEADS):
        ksl = slice(h * M_DQK, (h + 1) * M_DQK)
        vsl = slice(h * M_DV, (h + 1) * M_DV)
        m_prev = m_ref[h:h + 1, 0:1]
        bc = b_col[:, h:h + 1]
        log_d = jnp.where(tril, bc - b_row[h:h + 1, :] + li_row[h:h + 1, :], -jnp.inf)
        inter = bc + m_prev
        s = jnp.maximum(inter, jnp.max(log_d, axis=1, keepdims=True))
        w_inter = jnp.exp(inter - s)
        qh = q_ref[:, ksl]
        qh_bf = qh.astype(BF16)
        kh = k_ref[:, ksl] * (M_DQK ** -0.5)
        vh_bf = v_ref[:, vsl].astype(BF16)
        qk = _dot_rt(qh_bf, kh.astype(BF16)) * jnp.exp(log_d - s)
        c_prev = c_ref[h]
        n_prev = n_ref[h:h + 1, :]
        num = _dot(qk.astype(BF16), vh_bf) + w_inter * _dot(qh_bf, c_prev.astype(BF16))
        den = jnp.sum(qk, axis=1, keepdims=True) + w_inter * jnp.sum(qh * n_prev, axis=1, keepdims=True)
        hm = num / jnp.maximum(jnp.abs(den), jnp.exp(-s))
        hn = hm * lax.rsqrt(jnp.mean(hm * hm, axis=-1, keepdims=True) + EPS) * mnorm_ref[:, vsl]
        y_ref[:, vsl] = (hn * jax.nn.sigmoid(o_ref[:, vsl])).astype(y_ref.dtype)

        m_new = s[q - 1:q, :]
        wk = jnp.exp(b_last[:, h:h + 1] - bc + li_col[:, h:h + 1] - m_new)
        decay = jnp.exp(b_last[:, h:h + 1] + m_prev - m_new)
        kw = kh * wk
        c_ref[h] = decay * c_prev + _dot_lt(kw.astype(BF16), vh_bf)
        n_ref[h:h + 1, :] = decay * n_prev + jnp.sum(kw, axis=0, keepdims=True)
        m_ref[h:h + 1, :] = jnp.broadcast_to(m_new, (1, m_ref.shape[1]))


def _ml_prompt(proj, n_batch, n_chunks, n_rows_total, p):
    q = CHUNK
    rb = lambda b, c: b * n_chunks + c

    def pspec(width, name):
        return pl.BlockSpec((q, width), lambda b, c: (rb(b, c), PROJ_SRC[name][1] // width))

    def wspec(shape):
        return pl.BlockSpec(shape, lambda b, c: (0,) * len(shape))

    return pl.pallas_call(
        _ml_prompt_kernel,
        grid=(n_batch, n_chunks),
        in_specs=[pspec(1024, "q"), pspec(1024, "k"), pspec(2048, "v"), pspec(2048, "o"), pspec(SM_W, "s"),
                  wspec((1, 8)), wspec((8, 1)), wspec((1, 8)), wspec((8, 1)), wspec((1, 2048))],
        out_specs=[pl.BlockSpec((q, D_MODEL), lambda b, c: (rb(b, c), 0)),
                   pl.BlockSpec((None, M_HEADS, M_DQK, M_DV), lambda b, c: (b, 0, 0, 0)),
                   pl.BlockSpec((None, M_HEADS, M_DQK), lambda b, c: (b, 0, 0)),
                   pl.BlockSpec((None, M_HEADS, 128), lambda b, c: (b, 0, 0))],
        out_shape=[jax.ShapeDtypeStruct((n_rows_total, D_MODEL), BF16),
                   jax.ShapeDtypeStruct((n_batch, M_HEADS, M_DQK, M_DV), F32),
                   jax.ShapeDtypeStruct((n_batch, M_HEADS, M_DQK), F32),
                   jax.ShapeDtypeStruct((n_batch, M_HEADS, 128), F32)],
        compiler_params=_cp("parallel", "arbitrary"),
        name="ml_prompt",
    )(*_proj_arrays(proj, "qkvos"),
      p["ib_row"], p["ib_col"], p["fb_row"], p["fb_col"], p["m_norm"])


def _ml_sample_kernel(q_ref, k_ref, v_ref, o_ref, sm_ref, ib_row_ref, fb_row_ref, mnorm_ref,
                      cin_ref, nin_ref, min_ref,
                      y_ref, cout_ref, nout_ref, mout_ref):
    r8 = SAMPLE_ROWS_PER_STEP
    li = sm_ref[:, S_IG:S_IG + M_HEADS] + ib_row_ref[...]
    lf = _log_sigmoid(sm_ref[:, S_FG:S_FG + M_HEADS] + fb_row_ref[...])
    m_prev = min_ref[...]
    inter = lf + m_prev
    s = jnp.maximum(inter, li)
    w_inter = jnp.exp(inter - s)
    w_in = jnp.exp(li - s)
    mout_ref[...] = s
    row_k = lax.broadcasted_iota(jnp.int32, (r8, M_DQK), 0)

    for h in range(M_HEADS):
        ksl = slice(h * M_DQK, (h + 1) * M_DQK)
        vsl = slice(h * M_DV, (h + 1) * M_DV)
        qh = q_ref[:, ksl]
        kh = k_ref[:, ksl] * (M_DQK ** -0.5)
        vh = v_ref[:, vsl]
        vh_bf = vh.astype(BF16)
        n_prev = nin_ref[:, ksl]
        wi = w_inter[:, h:h + 1]
        qk = jnp.sum(qh * kh, axis=1, keepdims=True) * w_in[:, h:h + 1]
        kw = kh * w_in[:, h:h + 1]
        qc = jnp.zeros((r8, M_DV), F32)
        for r in range(r8):
            c_prev = cin_ref[r, h]
            q_r = jnp.where(row_k == r, qh, 0.0).astype(BF16)
            qc = qc + _dot(q_r, c_prev.astype(BF16))
            kw_r = jnp.where(row_k == r, kw, 0.0).astype(BF16)
            cout_ref[r, h] = w_inter[r:r + 1, h:h + 1] * c_prev + _dot_lt(kw_r, vh_bf)
        nout_ref[:, ksl] = wi * n_prev + kw
        num = qk * vh + wi * qc
        den = qk + wi * jnp.sum(qh * n_prev, axis=1, keepdims=True)
        hm = num / jnp.maximum(jnp.abs(den), jnp.exp(-s[:, h:h + 1]))
        hn = hm * lax.rsqrt(jnp.mean(hm * hm, axis=-1, keepdims=True) + EPS) * mnorm_ref[:, vsl]
        y_ref[:, vsl] = hn * jax.nn.sigmoid(o_ref[:, vsl])


def _ml_sample(proj, row0, n_dec, layer, c_all, n_all, m_all, p, prev):
    r8 = SAMPLE_ROWS_PER_STEP
    rb0 = row0 // r8

    def pspec(width, name):
        return pl.BlockSpec((r8, width), lambda i: (rb0 + i, PROJ_SRC[name][1] // width))

    def wspec(shape):
        return pl.BlockSpec(shape, lambda i: (0,) * len(shape))

    cspec = pl.BlockSpec((None, r8, M_HEADS, M_DQK, M_DV), lambda i: (layer, i, 0, 0, 0))
    nspec = pl.BlockSpec((None, r8, M_HEADS * M_DQK), lambda i: (layer, i, 0))
    mspec = pl.BlockSpec((None, r8, M_HEADS), lambda i: (layer, i, 0))
    in_specs = [pspec(1024, "q"), pspec(1024, "k"), pspec(2048, "v"), pspec(2048, "o"), pspec(SM_W, "s"),
                wspec((1, 8)), wspec((1, 8)), wspec((1, 2048)), cspec, nspec, mspec]
    args = [*_proj_arrays(proj, "qkvos"), p["ib_row"], p["fb_row"], p["m_norm"], c_all, n_all, m_all]
    n_in = len(args)
    aliases = {}
    if prev is not None:
        in_specs += [pl.BlockSpec(memory_space=pl.ANY)] * 3
        args += list(prev)
        aliases = {n_in: 1, n_in + 1: 2, n_in + 2: 3}
    return pl.pallas_call(
        _drop_refs(_ml_sample_kernel, n_in, len(aliases)),
        grid=(n_dec // r8,),
        in_specs=in_specs,
        out_specs=[pl.BlockSpec((r8, D_MODEL), lambda i: (i, 0)), cspec, nspec, mspec],
        out_shape=[jax.ShapeDtypeStruct((n_dec, D_MODEL), F32),
                   jax.ShapeDtypeStruct(c_all.shape, F32),
                   jax.ShapeDtypeStruct(n_all.shape, F32),
                   jax.ShapeDtypeStruct(m_all.shape, F32)],
        input_output_aliases=aliases,
        compiler_params=_cp("parallel"),
        name="ml_sample",
    )(*args)


FFN_TN = 512
FFN_SUB = 256


def _ffn_gate_prompt_kernel(a_ref, wg_ref, wu_ref, cw_ref, cb_ref, act_ref, tail_ref, *gpads):
    seq = a_ref.shape[0]
    a = a_ref[...]
    for h, gpad in enumerate(gpads):
        sl = slice(h * FFN_SUB, (h + 1) * FFN_SUB)
        g = _dot(a, wg_ref[:, sl].astype(BF16))
        u = _dot(a, wu_ref[:, sl].astype(BF16))
        gpad[0:8, :] = jnp.zeros((8, FFN_SUB), F32)
        gpad[8:seq + 8, :] = g
        acc = gpad[6:seq + 6, :] * cw_ref[0:1, sl]
        acc = acc + gpad[7:seq + 7, :] * cw_ref[1:2, sl]
        acc = acc + g * cw_ref[2:3, sl]
        act_ref[:, sl] = (_silu(acc + cb_ref[:, sl]) * u).astype(act_ref.dtype)
        tail_ref[:, sl] = g[seq - 8:seq, :]


def _ffn_gate_prompt(h2, w_gate, w_up, layer, n_batch, seq, n_rows_total, p):
    tn = FFN_TN
    k = h2.shape[1]
    wspec = pl.BlockSpec((None, k, tn), lambda b, j: (layer, 0, j))
    return pl.pallas_call(
        _ffn_gate_prompt_kernel,
        grid=(n_batch, D_FF // tn),
        in_specs=[_lhs_spec(seq, k), wspec, wspec,
                  pl.BlockSpec((3, tn), lambda b, j: (0, j)),
                  pl.BlockSpec((1, tn), lambda b, j: (0, j))],
        out_specs=[pl.BlockSpec((seq, tn), lambda b, j: (b, j)),
                   pl.BlockSpec((None, 8, tn), lambda b, j: (b, 0, j))],
        out_shape=[jax.ShapeDtypeStruct((n_rows_total, D_FF), BF16),
                   jax.ShapeDtypeStruct((n_batch, 8, D_FF), F32)],
        scratch_shapes=[pltpu.VMEM((seq + 8, FFN_SUB), F32)] * (tn // FFN_SUB),
        compiler_params=_cp("parallel", "arbitrary"),
        name="ffn_gate_prompt",
    )(h2, w_gate, w_up, p["ffn_cw"], p["ffn_cb"])


def _ffn_gate_sample_kernel(a_ref, wg_ref, wu_ref, s0_ref, s1_ref, cw_ref, cb_ref, act_any, act_ref, g_ref):
    del act_any
    a = a_ref[...]
    g = _dot(a, wg_ref[...].astype(BF16))
    u = _dot(a, wu_ref[...].astype(BF16))
    acc = s0_ref[...] * cw_ref[0:1, :]
    acc = acc + s1_ref[...] * cw_ref[1:2, :]
    acc = acc + g * cw_ref[2:3, :]
    act_ref[...] = (_silu(acc + cb_ref[...]) * u).astype(act_ref.dtype)
    g_ref[...] = g


def _ffn_gate_sample(h2, w_gate, w_up, layer, row0, n_dec, conv_state_all, p, act):
    tn = FFN_TN
    k = h2.shape[1]
    rb = row0 // n_dec
    nct = D_FF // tn
    wspec = pl.BlockSpec((None, k, tn), lambda j: (layer, 0, j))
    return pl.pallas_call(
        _ffn_gate_sample_kernel,
        grid=(nct,),
        in_specs=[pl.BlockSpec((n_dec, k), lambda j: (rb, 0)), wspec, wspec,
                  pl.BlockSpec((None, n_dec, tn), lambda j: (layer, 0, j)),
                  pl.BlockSpec((None, n_dec, tn), lambda j: (layer, 0, nct + j)),
                  pl.BlockSpec((3, tn), lambda j: (0, j)),
                  pl.BlockSpec((1, tn), lambda j: (0, j)),
                  pl.BlockSpec(memory_space=pl.ANY)],
        out_specs=[pl.BlockSpec((n_dec, tn), lambda j: (rb, j)),
                   pl.BlockSpec((n_dec, tn), lambda j: (0, j))],
        out_shape=[jax.ShapeDtypeStruct(act.shape, act.dtype),
                   jax.ShapeDtypeStruct((n_dec, D_FF), F32)],
        input_output_aliases={7: 0},
        compiler_params=_cp("arbitrary"),
        name="ffn_gate_sample",
    )(h2, w_gate, w_up, conv_state_all, conv_state_all, p["ffn_cw"], p["ffn_cb"], act)


def _pick_tile(m, candidates):
    for t in candidates:
        if m % t == 0:
            return t
    raise ValueError(f"no row tile for {m}")


def _layer_params(l, ssd_conv_w, ssd_conv_b, ssd_dt_bias, ssd_a_log, ssd_d, ssd_norm, gm_norm, gm_ws, gm_bs,
                  m_i_bias, m_f_bias, m_norm, ffn_conv_w, ffn_conv_b):
    cw, cb = ssd_conv_w[l], ssd_conv_b[l]
    return dict(
        cwx=cw[:, 0:2048], cbx=cb[None, 0:2048], cwb=cw[:, 2048:2560], cbb=cb[None, 2048:2560],
        cwc=cw[:, 2560:3072], cbc=cb[None, 2560:3072],
        dtb_row=ssd_dt_bias[l][None, :], dtb_col=ssd_dt_bias[l][:, None],
        alog_row=ssd_a_log[l][None, :], alog_col=ssd_a_log[l][:, None],
        dfull=jnp.repeat(ssd_d[l], SSD_HEAD_DIM)[None, :], ssd_norm=ssd_norm[l][None, :],
        gm_norm=gm_norm[l][None, :], gm_ws=gm_ws[l], gm_bst=gm_bs[l].T,
        gm_w0=jnp.repeat(gm_ws[l][:, 0, 0], GM_GROUP_DIM)[None, :],
        gm_b0=jnp.repeat(gm_bs[l][:, 0], GM_GROUP_DIM)[None, :],
        ib_row=m_i_bias[l][None, :], ib_col=m_i_bias[l][:, None],
        fb_row=m_f_bias[l][None, :], fb_col=m_f_bias[l][:, None], m_norm=m_norm[l][None, :],
        ffn_cw=ffn_conv_w[l], ffn_cb=ffn_conv_b[l][None, :],
    )


def kernel(x_prompt, x_sample, state_ssd, state_ssd_conv, state_mlstm_c, state_mlstm_n, state_mlstm_m,
           state_ffn_conv, norm1, w_in, ssd_conv_w, ssd_conv_b, ssd_dt_bias, ssd_a_log, ssd_d, ssd_norm,
           gm_norm, gm_ws, gm_bs, m_i_bias, m_f_bias, m_norm, w_branch, w_out, norm2,
           w_gate, w_up, ffn_conv_w, ffn_conv_b, w_down, final_norm):
    n_batch, seq, d = x_prompt.shape
    n_dec = x_sample.shape[0]
    depth = w_in.shape[0]
    n_chunks = seq // CHUNK
    n_prompt = n_batch * seq
    n_rows = n_prompt + n_dec
    tm = _pick_tile(n_rows, (832, 640, 384, 128))
    tm_wide = _pick_tile(n_rows, (1664, 832, 640, 384, 128))
    tm_mid = _pick_tile(n_rows, (2080, 1664, 832, 640, 384, 128))
    tm_big = _pick_tile(n_rows, (4160, 2080, 1664, 832, 640, 384, 128))
    w_in_t = jnp.swapaxes(w_in, 1, 2)

    ssd_conv_all = state_ssd_conv.reshape(depth, n_dec, -1)
    ssd_h_all = state_ssd.reshape(depth, n_dec, D_MODEL, SSD_STATE)
    ml_n_all = state_mlstm_n.reshape(depth, n_dec, -1)
    ffn_conv_all = state_ffn_conv.reshape(depth, n_dec, -1)
    s_ssd = None
    s_ml = None

    def xbc(rows):
        c0 = PROJ_SRC["x"][1]
        return rows[..., c0:c0 + 3072]

    def tail_rows(a, n_tail):
        return jnp.stack([a[(b + 1) * seq - n_tail:(b + 1) * seq] for b in range(n_batch)])

    x = jnp.concatenate([x_prompt.reshape(n_prompt, d), x_sample.reshape(n_dec, d)], axis=0)
    outs = [[] for _ in range(13)]
    for l in range(depth):
        p = _layer_params(l, ssd_conv_w, ssd_conv_b, ssd_dt_bias, ssd_a_log, ssd_d, ssd_norm, gm_norm, gm_ws, gm_bs,
                          m_i_bias, m_f_bias, m_norm, ffn_conv_w, ffn_conv_b)
        h = _rmsnorm_rows(x, norm1[l], BF16, tm)
        proj = {"a": _matmul_t(h, w_in_t, l, tm_big, 256, *W_IN_A),
                "b": _matmul_t(h, w_in_t, l, tm_big, 256, *W_IN_B),
                "g": _matmul_t(h, w_in_t, l, tm_big, 256, *W_IN_G),
                "s": _matmul_small(h, w_in_t, l, tm_mid)}

        y0, p_ssd = _ssd_prompt(proj, n_batch, n_chunks, n_rows, p)
        y1 = _gm_prompt(proj, n_prompt, n_rows, p)
        y2, p_c, p_n, p_m = _ml_prompt(proj, n_batch, n_chunks, n_rows, p)

        ys0, s_ssd = _ssd_sample(proj, n_prompt, n_dec, l, ssd_conv_all, ssd_h_all, p, s_ssd)
        ys2, *s_ml = _ml_sample(proj, n_prompt, n_dec, l, state_mlstm_c, ml_n_all, state_mlstm_m, p, s_ml)
        y0, y1, y2, s_vn = _gm_sample_pack(proj, n_prompt, n_dec, p, ys0, ys2, y0, y1, y2)

        merged = _merge(y0, y1, y2, w_branch, l, proj, tm_wide, 256)
        x = _matmul(merged, w_out, l, tm_mid, 512, residual=x)

        h2 = _rmsnorm_rows(x, norm2[l], BF16, tm)
        act, g_tail = _ffn_gate_prompt(h2, w_gate, w_up, l, n_batch, seq, n_rows, p)
        act, g_dec = _ffn_gate_sample(h2, w_gate, w_up, l, n_prompt, n_dec, ffn_conv_all, p, act)
        x = _matmul(act, w_down, l, tm_wide, 256, residual=x)

        outs[0].append(p_ssd.reshape(n_batch, SSD_HEADS, SSD_HEAD_DIM, SSD_STATE))
        outs[1].append(xbc(tail_rows(proj["a"], 3)))
        outs[2].append(p_c)
        outs[3].append(p_n)
        outs[4].append(p_m[:, :, 0])
        outs[5].append(g_tail[:, 6:8, :])
        outs[7].append(jnp.concatenate([state_ssd_conv[l][:, 1:], xbc(proj["a"][n_prompt:])[:, None, :]], axis=1))
        outs[11].append(jnp.concatenate([state_ffn_conv[l][:, 1:], g_dec[:, None, :]], axis=1))
        outs[12].append(s_vn.reshape(n_dec, 1, D_MODEL))

    tf = _pick_tile(n_prompt, (1024, 512, 256, 128))
    y_prompt = _rmsnorm_rows(x, final_norm, F32, tf, 0, n_prompt).reshape(n_batch, seq, d)
    y_sample = _rmsnorm_rows(x, final_norm, F32, n_dec, n_prompt // n_dec, n_dec).reshape(n_dec, 1, d)
    s_c, s_n, s_m = s_ml
    stacked = {6: s_ssd.reshape(depth, n_dec, SSD_HEADS, SSD_HEAD_DIM, SSD_STATE), 8: s_c,
               9: s_n.reshape(depth, n_dec, M_HEADS, M_DQK), 10: s_m}
    return (y_prompt, y_sample) + tuple(stacked[i] if i in stacked else jnp.stack(outs[i]) for i in range(13))
```

```python
import functools

import jax
import jax.numpy as jnp
from jax import lax
from jax.experimental import pallas as pl
from jax.experimental.pallas import tpu as pltpu

F32 = jnp.float32
BF16 = jnp.bfloat16

D_MODEL = 2048
DEPTH = 4
CHUNK = 128
EPS = 1e-6
SSD_HEADS = 32
SSD_HEAD_DIM = 64
SSD_GROUPS = 4
SSD_STATE = 128
SSD_GROUP_COLS = D_MODEL // SSD_GROUPS
GM_GROUPS = 8
GM_GROUP_DIM = 256
M_HEADS = 8
M_DV = 256
M_DQK = 128
D_FF = 5632

PROJ_SRC = {"z": ("a", 0), "x": ("a", 2048), "B": ("a", 4096), "C": ("a", 4608),
            "u": ("b", 0), "vg": ("b", 2048), "q": ("b", 4096), "k": ("b", 5120), "v": ("b", 6144), "o": ("b", 8192),
            "g": ("g", 0), "s": ("s", 0)}
W_IN_A = (0, 5120)
W_IN_B = (5152, 10240)
W_IN_G = (15408, 6144)
W_IN_DT_COL = 5120
W_IN_IF_COL = 15392
SM_W = 256
S_DT = 0
S_IG = 128
S_FG = 136

VMEM_LIMIT_BYTES = 56 * 1024 * 1024
SAMPLE_ROWS_PER_STEP = 8


def _cp(*sem):
    return pltpu.CompilerParams(dimension_semantics=sem, vmem_limit_bytes=VMEM_LIMIT_BYTES)


def _dot(a, b):
    return jnp.dot(a, b, preferred_element_type=F32)


def _dot_rt(a, b):
    return lax.dot_general(a, b, (((1,), (1,)), ((), ())), preferred_element_type=F32)


def _dot_lt(a, b):
    return lax.dot_general(a, b, (((0,), (0,)), ((), ())), preferred_element_type=F32)


def _split3(a):
    a1 = a.astype(BF16)
    r1 = a - a1.astype(F32)
    a2 = r1.astype(BF16)
    a3 = (r1 - a2.astype(F32)).astype(BF16)
    return a1, a2, a3


def _dot3_l(a_f32, b_bf16):
    a1, a2, a3 = _split3(a_f32)
    return (_dot(a1, b_bf16) + _dot(a2, b_bf16)) + _dot(a3, b_bf16)


def _dot3_r(a_bf16, b_f32):
    b1, b2, b3 = _split3(b_f32)
    return (_dot(a_bf16, b1) + _dot(a_bf16, b2)) + _dot(a_bf16, b3)


def _tri(n, lower=True):
    r = lax.broadcasted_iota(jnp.int32, (n, n), 0)
    c = lax.broadcasted_iota(jnp.int32, (n, n), 1)
    return (r >= c) if lower else (r <= c)


LOG2E = 1.4426950408889634


def _softplus(x):
    return jnp.maximum(x, 0.0) + jnp.log1p(jnp.exp(-jnp.abs(x)))


def _log_sigmoid(x):
    return -_softplus(-x)


def _silu(x):
    return x * jax.nn.sigmoid(x)


def _rms_kernel(x_ref, g_ref, o_ref):
    x = x_ref[...]
    y = x * lax.rsqrt(jnp.mean(x * x, axis=-1, keepdims=True) + EPS)
    o_ref[...] = (y * g_ref[...]).astype(o_ref.dtype)


def _rmsnorm_rows(x, g, out_dtype, tm, row_block0=0, n_rows=None):
    m_total, d = x.shape
    n_rows = m_total if n_rows is None else n_rows
    return pl.pallas_call(
        _rms_kernel,
        grid=(n_rows // tm,),
        in_specs=[pl.BlockSpec((tm, d), lambda i: (row_block0 + i, 0)),
                  pl.BlockSpec((1, d), lambda i: (0, 0))],
        out_specs=pl.BlockSpec((tm, d), lambda i: (i, 0)),
        out_shape=jax.ShapeDtypeStruct((n_rows, d), out_dtype),
        compiler_params=_cp("parallel"),
        name="rmsnorm_rows",
    )(x, g.reshape(1, d))


def _mm_kernel(has_res, *refs):
    a_ref, w_ref = refs[0], refs[1]
    acc = _dot(a_ref[...], w_ref[...].astype(BF16))
    if has_res:
        acc = refs[2][...] + acc
    o_ref = refs[-1]
    o_ref[...] = acc.astype(o_ref.dtype)


SINGLE_BUFFER_LHS_BYTES = 6 * 1024 * 1024


def _lhs_spec(tm, k):
    if tm * k * 2 >= SINGLE_BUFFER_LHS_BYTES:
        return pl.BlockSpec((tm, k), lambda i, j: (i, 0), pipeline_mode=pl.Buffered(1))
    return pl.BlockSpec((tm, k), lambda i, j: (i, 0))


def _matmul(a, w, layer, tm, tn, residual=None):
    m, k = a.shape
    n = w.shape[-1]
    in_specs = [_lhs_spec(tm, k),
                pl.BlockSpec((None, k, tn), lambda i, j: (layer, 0, j))]
    args = [a, w]
    if residual is not None:
        in_specs.append(pl.BlockSpec((tm, tn), lambda i, j: (i, j)))
        args.append(residual)
    return pl.pallas_call(
        functools.partial(_mm_kernel, residual is not None),
        grid=(m // tm, n // tn),
        in_specs=in_specs,
        out_specs=pl.BlockSpec((tm, tn), lambda i, j: (i, j)),
        out_shape=jax.ShapeDtypeStruct((m, n), F32),
        compiler_params=_cp("parallel", "arbitrary"),
        name="matmul_res" if residual is not None else "matmul",
    )(*args)


def _mm_t_kernel(a_ref, wt_ref, o_ref):
    o_ref[...] = _dot_rt(a_ref[...], wt_ref[0].astype(BF16))


def _wt_spec(layer, rows, k, row_of_step):
    def index_map(*idx):
        r = row_of_step(*idx)
        return (layer, r if isinstance(r, int) else pl.multiple_of(r, 8), 0)

    return pl.BlockSpec((pl.Element(1), pl.Element(rows), pl.Element(k)), index_map)


def _matmul_t(a, wt, layer, tm, tn, col0, n):
    m, k = a.shape
    return pl.pallas_call(
        _mm_t_kernel,
        grid=(m // tm, n // tn),
        in_specs=[_lhs_spec(tm, k),
                  _wt_spec(layer, tn, k, lambda i, j: col0 + tn * j)],
        out_specs=pl.BlockSpec((tm, tn), lambda i, j: (i, j)),
        out_shape=jax.ShapeDtypeStruct((m, n), F32),
        compiler_params=_cp("parallel", "arbitrary"),
        name="matmul_t",
    )(a, wt)


def _mm_small_kernel(a_ref, wdt_ref, wif_ref, o_ref):
    a = a_ref[...]
    o_ref[...] = jnp.zeros(o_ref.shape, F32)
    o_ref[:, S_DT:S_DT + SSD_HEADS] = _dot_rt(a, wdt_ref[0].astype(BF16))
    o_ref[:, S_IG:S_IG + 2 * M_HEADS] = _dot_rt(a, wif_ref[0].astype(BF16))


def _matmul_small(a, wt, layer, tm):
    m, k = a.shape
    return pl.pallas_call(
        _mm_small_kernel,
        grid=(m // tm,),
        in_specs=[pl.BlockSpec((tm, k), lambda i: (i, 0)),
                  _wt_spec(layer, SSD_HEADS, k, lambda i: W_IN_DT_COL),
                  _wt_spec(layer, 2 * M_HEADS, k, lambda i: W_IN_IF_COL)],
        out_specs=pl.BlockSpec((tm, SM_W), lambda i: (i, 0)),
        out_shape=jax.ShapeDtypeStruct((m, SM_W), F32),
        compiler_params=_cp("parallel"),
        name="matmul_small",
    )(a, wt, wt)


def _merge_kernel(y0_ref, y1_ref, y2_ref, w_ref, g0_ref, g1_ref, g2_ref, o_ref):
    acc = jax.nn.sigmoid(g0_ref[...]) * _dot(y0_ref[...], w_ref[0].astype(BF16))
    acc = acc + jax.nn.sigmoid(g1_ref[...]) * _dot(y1_ref[...], w_ref[1].astype(BF16))
    acc = acc + jax.nn.sigmoid(g2_ref[...]) * _dot(y2_ref[...], w_ref[2].astype(BF16))
    o_ref[...] = acc.astype(o_ref.dtype)


def _merge(y0, y1, y2, w_branch, layer, proj, tm, tn):
    m = y0.shape[0]
    gb = 0
    gstep = D_MODEL // tn
    yspec = _lhs_spec(tm, D_MODEL)

    def gspec(b):
        return pl.BlockSpec((tm, tn), lambda i, j: (i, gb + b * gstep + j))

    return pl.pallas_call(
        _merge_kernel,
        grid=(m // tm, D_MODEL // tn),
        in_specs=[yspec, yspec, yspec,
                  pl.BlockSpec((None, 3, D_MODEL, tn), lambda i, j: (layer, 0, 0, j)),
                  gspec(0), gspec(1), gspec(2)],
        out_specs=pl.BlockSpec((tm, tn), lambda i, j: (i, j)),
        out_shape=jax.ShapeDtypeStruct((m, D_MODEL), BF16),
        compiler_params=_cp("parallel", "arbitrary"),
        name="merge",
    )(y0, y1, y2, w_branch, proj["g"], proj["g"], proj["g"])


def _pair_select(lane_lo, col_a, col_b):
    return jnp.where(lane_lo, col_a, col_b)


def _ssd_prompt_kernel(z_ref, x_ref, b_ref, c_ref, sm_ref,
                       cwx_ref, cbx_ref, cwb_ref, cbb_ref, cwc_ref, cbc_ref,
                       dtb_row_ref, dtb_col_ref, alog_row_ref, alog_col_ref, dfull_ref, norm_ref,
                       y_ref, hout_ref,
                       xpad, bpad, cpad, ht_scr, ybuf, side_work=None):
    c = pl.program_id(1)
    nc = pl.num_programs(1)
    q = CHUNK

    @pl.when(c == 0)
    def _():
        xpad[0:8, :] = jnp.zeros((8, xpad.shape[1]), F32)
        bpad[0:8, :] = jnp.zeros((8, bpad.shape[1]), F32)
        cpad[0:8, :] = jnp.zeros((8, cpad.shape[1]), F32)
        ht_scr[...] = jnp.zeros(ht_scr.shape, F32)

    @pl.when(c > 0)
    def _():
        xpad[0:8, :] = xpad[q:q + 8, :]
        bpad[0:8, :] = bpad[q:q + 8, :]
        cpad[0:8, :] = cpad[q:q + 8, :]

    if side_work is not None:
        side_work()

    xpad[8:q + 8, :] = x_ref[...]
    bpad[8:q + 8, :] = b_ref[...]
    cpad[8:q + 8, :] = c_ref[...]

    def conv(pad, w_ref, bias_ref, lanes):
        acc = pad[5:q + 5, lanes] * w_ref[0:1, lanes]
        acc = acc + pad[6:q + 6, lanes] * w_ref[1:2, lanes]
        acc = acc + pad[7:q + 7, lanes] * w_ref[2:3, lanes]
        acc = acc + pad[8:q + 8, lanes] * w_ref[3:4, lanes]
        return _silu(acc + bias_ref[:, lanes])

    bm = conv(bpad, cwb_ref, cbb_ref, slice(None))
    cm = conv(cpad, cwc_ref, cbc_ref, slice(None))

    sm = sm_ref[...]
    sm_t = sm.T
    dt_col = _softplus(sm[:, S_DT:S_DT + SSD_HEADS] + dtb_row_ref[...])
    dt_row = _softplus(sm_t[S_DT:S_DT + SSD_HEADS, :] + dtb_col_ref[...])
    da_col = dt_col * (-jnp.exp(alog_row_ref[...]))
    da_row = dt_row * (-jnp.exp(alog_col_ref[...]))
    tril = _tri(q, True)
    s_col = _dot3_r(tril.astype(BF16), da_col)
    s_row = _dot3_l(da_row, _tri(q, False).astype(BF16))
    s2_col = s_col * LOG2E
    s2_row = s_row * LOG2E - jnp.log2(dt_row)
    es_col = jnp.exp(s_col)
    s_last = s_col[q - 1:q, :]
    wend_col = jnp.exp(s_last - s_col) * dt_col
    elast = jnp.exp(s_last)

    lane_lo = lax.broadcasted_iota(jnp.int32, (q, 128), 1) < SSD_HEAD_DIM
    lane_lo1 = lax.broadcasted_iota(jnp.int32, (1, 128), 1) < SSD_HEAD_DIM

    for g in range(SSD_GROUPS):
        bg = bm[:, g * SSD_STATE:(g + 1) * SSD_STATE]
        cg = cm[:, g * SSD_STATE:(g + 1) * SSD_STATE].astype(BF16)
        bg_t = bg.T.astype(BF16)
        cb = _dot(cg, bg_t)
        gsl = slice(g * SSD_GROUP_COLS, (g + 1) * SSD_GROUP_COLS)
        yint = _dot(cg, ht_scr[:, gsl].astype(BF16))
        for jj in range(4):
            j = g * 4 + jj
            ha, hb = 2 * j, 2 * j + 1
            psl = slice(j * 128, (j + 1) * 128)
            xp = conv(xpad, cwx_ref, cbx_ref, psl)
            xp_bf = xp.astype(BF16)
            ys = []
            for h in (ha, hb):
                dec_dt = jnp.exp2(jnp.where(tril, s2_col[:, h:h + 1] - s2_row[h:h + 1, :], -jnp.inf))
                ys.append(_dot((cb * dec_dt).astype(BF16), xp_bf))
            y = jnp.where(lane_lo, ys[0], ys[1])
            y = y + yint[:, jj * 128:(jj + 1) * 128] * _pair_select(lane_lo, es_col[:, ha:ha + 1], es_col[:, hb:hb + 1])
            y = y + dfull_ref[:, psl] * xp
            ybuf[:, psl] = y * _silu(z_ref[:, psl])
            xw = xp * _pair_select(lane_lo, wend_col[:, ha:ha + 1], wend_col[:, hb:hb + 1])
            dpair = _pair_select(lane_lo1, elast[:, ha:ha + 1], elast[:, hb:hb + 1])
            ht_scr[:, psl] = ht_scr[:, psl] * dpair + _dot(bg_t, xw.astype(BF16))
        yg = ybuf[:, gsl]
        yn = yg * lax.rsqrt(jnp.mean(yg * yg, axis=-1, keepdims=True) + EPS)
        y_ref[:, gsl] = (yn * norm_ref[:, gsl]).astype(y_ref.dtype)

    @pl.when(c == nc - 1)
    def _():
        hout_ref[...] = ht_scr[...].T


def _proj_arrays(proj, names):
    return [proj[PROJ_SRC[n][0]] for n in names]


N_SSD_IN, N_SSD_OUT, N_ML_IN, N_ML_OUT = 17, 2, 10, 4


def _ssd_ml_prompt_kernel(*refs):
    ssd_in = refs[:N_SSD_IN]
    ml_in = refs[N_SSD_IN:N_SSD_IN + N_ML_IN]
    pos = N_SSD_IN + N_ML_IN
    ssd_out = refs[pos:pos + N_SSD_OUT]
    ml_out = refs[pos + N_SSD_OUT:pos + N_SSD_OUT + N_ML_OUT]
    ssd_scratch = refs[pos + N_SSD_OUT + N_ML_OUT:]
    _ssd_prompt_kernel(*ssd_in, *ssd_out, *ssd_scratch,
                       side_work=functools.partial(_ml_prompt_kernel, *ml_in, *ml_out))


def _prompt_scans(proj, n_batch, n_chunks, n_rows_total, p):
    ssd = _ssd_prompt_specs(proj, n_chunks, n_batch, n_rows_total, p)
    ml = _ml_prompt_specs(proj, n_chunks, n_batch, n_rows_total, p)
    assert (len(ssd["args"]), len(ssd["out_shape"]), len(ml["args"]), len(ml["out_shape"])) == (
        N_SSD_IN, N_SSD_OUT, N_ML_IN, N_ML_OUT)
    return pl.pallas_call(
        _ssd_ml_prompt_kernel,
        grid=(n_batch, n_chunks),
        in_specs=ssd["in_specs"] + ml["in_specs"],
        out_specs=ssd["out_specs"] + ml["out_specs"],
        out_shape=ssd["out_shape"] + ml["out_shape"],
        scratch_shapes=ssd["scratch"],
        compiler_params=_cp("parallel", "arbitrary"),
        name="ssd_ml_prompt",
    )(*ssd["args"], *ml["args"])


def _ssd_prompt_specs(proj, n_chunks, n_batch, n_rows_total, p):
    q = CHUNK
    rb = lambda b, c: b * n_chunks + c

    def pspec(width, name):
        return pl.BlockSpec((q, width), lambda b, c: (rb(b, c), PROJ_SRC[name][1] // width))

    def wspec(shape):
        return pl.BlockSpec(shape, lambda b, c: (0,) * len(shape))

    return dict(
        in_specs=[pspec(2048, "z"), pspec(2048, "x"), pspec(512, "B"), pspec(512, "C"), pspec(SM_W, "s"),
                  wspec((4, 2048)), wspec((1, 2048)), wspec((4, 512)), wspec((1, 512)), wspec((4, 512)), wspec((1, 512)),
                  wspec((1, 32)), wspec((32, 1)), wspec((1, 32)), wspec((32, 1)), wspec((1, 2048)), wspec((1, 2048))],
        args=[*_proj_arrays(proj, "zxBCs"),
              p["cwx"], p["cbx"], p["cwb"], p["cbb"], p["cwc"], p["cbc"],
              p["dtb_row"], p["dtb_col"], p["alog_row"], p["alog_col"], p["dfull"], p["ssd_norm"]],
        out_specs=[pl.BlockSpec((q, D_MODEL), lambda b, c: (rb(b, c), 0)),
                   pl.BlockSpec((None, D_MODEL, SSD_STATE), lambda b, c: (b, 0, 0))],
        out_shape=[jax.ShapeDtypeStruct((n_rows_total, D_MODEL), BF16),
                   jax.ShapeDtypeStruct((n_batch, D_MODEL, SSD_STATE), F32)],
        scratch=[pltpu.VMEM((q + 8, 2048), F32), pltpu.VMEM((q + 8, 512), F32), pltpu.VMEM((q + 8, 512), F32),
                 pltpu.VMEM((SSD_STATE, D_MODEL), F32), pltpu.VMEM((q, D_MODEL), F32)])


def _ssd_sample_kernel(z_ref, x_ref, b_ref, c_ref, sm_ref, cs0_ref, cs1_ref, cs2_ref,
                       cwx_ref, cbx_ref, cwb_ref, cbb_ref, cwc_ref, cbc_ref,
                       dtb_row_ref, alog_row_ref, dfull_ref, norm_ref, hin_ref,
                       y_ref, hout_ref):
    r8 = SAMPLE_ROWS_PER_STEP

    def conv(lo, hi, new, w_ref, bias_ref):
        acc = cs0_ref[:, lo:hi] * w_ref[0:1, :]
        acc = acc + cs1_ref[:, lo:hi] * w_ref[1:2, :]
        acc = acc + cs2_ref[:, lo:hi] * w_ref[2:3, :]
        acc = acc + new * w_ref[3:4, :]
        return _silu(acc + bias_ref[...])

    xs = conv(0, 2048, x_ref[...], cwx_ref, cbx_ref)
    bm = conv(2048, 2560, b_ref[...], cwb_ref, cbb_ref)
    cm = conv(2560, 3072, c_ref[...], cwc_ref, cbc_ref)
    dt = _softplus(sm_ref[:, S_DT:S_DT + SSD_HEADS] + dtb_row_ref[...])
    e = jnp.exp(dt * (-jnp.exp(alog_row_ref[...])))
    hh = lax.broadcasted_iota(jnp.int32, (SSD_HEADS, D_MODEL), 0)
    cc = lax.broadcasted_iota(jnp.int32, (SSD_HEADS, D_MODEL), 1)
    expand = jnp.where((cc >= hh * SSD_HEAD_DIM) & (cc < (hh + 1) * SSD_HEAD_DIM), 1.0, 0.0).astype(BF16)
    dt_full = _dot3_l(dt, expand)
    dx = xs * dt_full
    row = lax.broadcasted_iota(jnp.int32, (r8, SSD_GROUP_COLS), 0)
    row_n = lax.broadcasted_iota(jnp.int32, (r8, SSD_STATE), 0)

    ygroups = []
    for g in range(SSD_GROUPS):
        gsl = slice(g * SSD_GROUP_COLS, (g + 1) * SSD_GROUP_COLS)
        nsl = slice(g * SSD_STATE, (g + 1) * SSD_STATE)
        bg = bm[:, nsl].astype(BF16)
        cg = cm[:, nsl]
        dxg = dx[:, gsl]
        yacc = jnp.zeros((r8, SSD_GROUP_COLS), F32)
        for r in range(r8):
            a_x = jnp.where(row == r, dxg, 0.0).astype(BF16)
            upd = _dot_lt(a_x, bg)
            hin = hin_ref[r, gsl, :]
            heads = []
            for hh in range(SSD_GROUP_COLS // SSD_HEAD_DIM):
                hrows = slice(hh * SSD_HEAD_DIM, (hh + 1) * SSD_HEAD_DIM)
                hcol = g * (SSD_GROUP_COLS // SSD_HEAD_DIM) + hh
                heads.append(hin[hrows, :] * e[r:r + 1, hcol:hcol + 1] + upd[hrows, :])
            hn = jnp.concatenate(heads, axis=0)
            hout_ref[r, gsl, :] = hn
            c_r = jnp.where(row_n == r, cg, 0.0).astype(BF16)
            yacc = yacc + _dot_rt(c_r, hn.astype(BF16))
        ygroups.append(yacc)

    for g in range(SSD_GROUPS):
        gsl = slice(g * SSD_GROUP_COLS, (g + 1) * SSD_GROUP_COLS)
        y = ygroups[g] + dfull_ref[:, gsl] * xs[:, gsl]
        y = y * _silu(z_ref[:, gsl])
        yn = y * lax.rsqrt(jnp.mean(y * y, axis=-1, keepdims=True) + EPS)
        y_ref[:, gsl] = yn * norm_ref[:, gsl]


def _drop_refs(kernel_fn, n_inputs, n_dropped):
    def body(*refs):
        return kernel_fn(*refs[:n_inputs], *refs[n_inputs + n_dropped:])
    return body


def _ssd_sample(proj, row0, n_dec, layer, conv_state_all, h_state_all, p, h_out_prev):
    r8 = SAMPLE_ROWS_PER_STEP
    rb0 = row0 // r8
    depth = h_state_all.shape[0]

    def pspec(width, name):
        return pl.BlockSpec((r8, width), lambda i: (rb0 + i, PROJ_SRC[name][1] // width))

    def wspec(shape):
        return pl.BlockSpec(shape, lambda i: (0,) * len(shape))

    def cspec(j):
        return pl.BlockSpec((None, r8, 3072), lambda i: (layer, i, j))

    hspec = pl.BlockSpec((None, r8, D_MODEL, SSD_STATE), lambda i: (layer, i, 0, 0))
    in_specs = [pspec(2048, "z"), pspec(2048, "x"), pspec(512, "B"), pspec(512, "C"), pspec(SM_W, "s"),
                cspec(0), cspec(1), cspec(2),
                wspec((4, 2048)), wspec((1, 2048)), wspec((4, 512)), wspec((1, 512)), wspec((4, 512)), wspec((1, 512)),
                wspec((1, 32)), wspec((1, 32)), wspec((1, 2048)), wspec((1, 2048)), hspec]
    args = [*_proj_arrays(proj, "zxBCs"), conv_state_all, conv_state_all, conv_state_all,
            p["cwx"], p["cbx"], p["cwb"], p["cbb"], p["cwc"], p["cbc"],
            p["dtb_row"], p["alog_row"], p["dfull"], p["ssd_norm"], h_state_all]
    n_in = len(args)
    aliases = {}
    if h_out_prev is not None:
        in_specs.append(pl.BlockSpec(memory_space=pl.ANY))
        args.append(h_out_prev)
        aliases = {n_in: 1}
    return pl.pallas_call(
        _drop_refs(_ssd_sample_kernel, n_in, len(aliases)),
        grid=(n_dec // r8,),
        in_specs=in_specs,
        out_specs=[pl.BlockSpec((r8, D_MODEL), lambda i: (i, 0)), hspec],
        out_shape=[jax.ShapeDtypeStruct((n_dec, D_MODEL), F32),
                   jax.ShapeDtypeStruct((depth, n_dec, D_MODEL, SSD_STATE), F32)],
        input_output_aliases=aliases,
        compiler_params=_cp("parallel"),
        name="ssd_sample",
    )(*args)


GM_ROWS_PER_STEP = 512


def _gm_prompt_kernel(u_ref, v_ref, gnorm_ref, ws_ref, bst_ref, y_ref):
    q = CHUNK
    tril = _tri(q, True)
    for cc in range(GM_ROWS_PER_STEP // q):
        rsl = slice(cc * q, (cc + 1) * q)
        u = jax.nn.gelu(u_ref[rsl, :])
        v = jax.nn.gelu(v_ref[rsl, :])
        vn = v * lax.rsqrt(jnp.mean(v * v, axis=-1, keepdims=True) + EPS) * gnorm_ref[...]
        for g in range(GM_GROUPS):
            gsl = slice(g * GM_GROUP_DIM, (g + 1) * GM_GROUP_DIM)
            w = jnp.where(tril, ws_ref[g], 0.0).astype(BF16)
            mixed = _dot(w, vn[:, gsl].astype(BF16)) + bst_ref[:, g:g + 1]
            y_ref[rsl, gsl] = (u[:, gsl] * mixed).astype(y_ref.dtype)


def _gm_prompt(proj, n_prompt_rows, n_rows_total, p):
    r = GM_ROWS_PER_STEP
    return pl.pallas_call(
        _gm_prompt_kernel,
        grid=(n_prompt_rows // r,),
        in_specs=[pl.BlockSpec((r, 2048), lambda i: (i, PROJ_SRC["u"][1] // 2048)),
                  pl.BlockSpec((r, 2048), lambda i: (i, PROJ_SRC["vg"][1] // 2048)),
                  pl.BlockSpec((1, 2048), lambda i: (0, 0)),
                  pl.BlockSpec((GM_GROUPS, CHUNK, CHUNK), lambda i: (0, 0, 0)),
                  pl.BlockSpec((CHUNK, GM_GROUPS), lambda i: (0, 0))],
        out_specs=pl.BlockSpec((r, D_MODEL), lambda i: (i, 0)),
        out_shape=jax.ShapeDtypeStruct((n_rows_total, D_MODEL), BF16),
        compiler_params=_cp("parallel"),
        name="gm_prompt",
    )(proj["b"], proj["b"], p["gm_norm"], p["gm_ws"], p["gm_bst"])


def _gm_sample_pack_kernel(u_ref, v_ref, gnorm_ref, w0_ref, b0_ref, yssd_ref, ym_ref,
                           y0_any, y1_any, y2_any,
                           o0_ref, o1_ref, o2_ref, vn_ref):
    del y0_any, y1_any, y2_any
    u = jax.nn.gelu(u_ref[...])
    v = jax.nn.gelu(v_ref[...])
    vn = v * lax.rsqrt(jnp.mean(v * v, axis=-1, keepdims=True) + EPS) * gnorm_ref[...]
    vn_ref[...] = vn
    mixed = w0_ref[...] * vn + b0_ref[...]
    o0_ref[...] = yssd_ref[...].astype(o0_ref.dtype)
    o1_ref[...] = (u * mixed).astype(o1_ref.dtype)
    o2_ref[...] = ym_ref[...].astype(o2_ref.dtype)


def _gm_sample_pack(proj, row0, n_dec, p, yssd_s, ym_s, y0, y1, y2):
    rb = row0 // n_dec
    full = lambda shape: pl.BlockSpec(shape, lambda i: (0,) * len(shape))
    anyspec = pl.BlockSpec(memory_space=pl.ANY)
    ospec = pl.BlockSpec((n_dec, D_MODEL), lambda i: (rb, 0))
    return pl.pallas_call(
        _gm_sample_pack_kernel,
        grid=(1,),
        in_specs=[pl.BlockSpec((n_dec, 2048), lambda i: (rb, PROJ_SRC["u"][1] // 2048)),
                  pl.BlockSpec((n_dec, 2048), lambda i: (rb, PROJ_SRC["vg"][1] // 2048)),
                  full((1, 2048)), full((1, 2048)), full((1, 2048)),
                  full((n_dec, D_MODEL)), full((n_dec, D_MODEL)),
                  anyspec, anyspec, anyspec],
        out_specs=[ospec, ospec, ospec, full((n_dec, D_MODEL))],
        out_shape=[jax.ShapeDtypeStruct(y0.shape, y0.dtype), jax.ShapeDtypeStruct(y1.shape, y1.dtype),
                   jax.ShapeDtypeStruct(y2.shape, y2.dtype), jax.ShapeDtypeStruct((n_dec, D_MODEL), F32)],
        input_output_aliases={7: 0, 8: 1, 9: 2},
        compiler_params=_cp("arbitrary"),
        name="gm_sample_pack",
    )(proj["b"], proj["b"], p["gm_norm"], p["gm_w0"], p["gm_b0"], yssd_s, ym_s, y0, y1, y2)


def _ml_prompt_kernel(q_ref, k_ref, v_ref, o_ref, sm_ref,
                      ib_row_ref, ib_col_ref, fb_row_ref, fb_col_ref, mnorm_ref,
                      y_ref, c_ref, n_ref, m_ref):
    c = pl.program_id(1)
    q = CHUNK

    @pl.when(c == 0)
    def _():
        c_ref[...] = jnp.zeros(c_ref.shape, F32)
        n_ref[...] = jnp.zeros(n_ref.shape, F32)
        m_ref[...] = jnp.zeros(m_ref.shape, F32)

    sm = sm_ref[...]
    sm_t = sm.T
    li_col = sm[:, S_IG:S_IG + M_HEADS] + ib_row_ref[...]
    lf_col = _log_sigmoid(sm[:, S_FG:S_FG + M_HEADS] + fb_row_ref[...])
    li_row = sm_t[S_IG:S_IG + M_HEADS, :] + ib_col_ref[...]
    lf_row = _log_sigmoid(sm_t[S_FG:S_FG + M_HEADS, :] + fb_col_ref[...])
    tril = _tri(q, True)
    b_col = _dot3_r(tril.astype(BF16), lf_col)
    b_row = _dot3_l(lf_row, _tri(q, False).astype(BF16))
    b_last = b_col[q - 1:q, :]

    for h in range(M_HEADS):
        ksl = slice(h * M_DQK, (h + 1) * M_DQK)
        vsl = slice(h * M_DV, (h + 1) * M_DV)
        m_prev = m_ref[h:h + 1, 0:1]
        bc = b_col[:, h:h + 1]
        log_d = jnp.where(tril, bc - b_row[h:h + 1, :] + li_row[h:h + 1, :], -jnp.inf)
        inter = bc + m_prev
        s = jnp.maximum(inter, jnp.max(log_d, axis=1, keepdims=True))
        w_inter = jnp.exp(inter - s)
        qh = q_ref[:, ksl]
        qh_bf = qh.astype(BF16)
        kh = k_ref[:, ksl] * (M_DQK ** -0.5)
        vh_bf = v_ref[:, vsl].astype(BF16)
        qk = _dot_rt(qh_bf, kh.astype(BF16)) * jnp.exp(log_d - s)
        c_prev = c_ref[h]
        n_prev = n_ref[h:h + 1, :]
        num = _dot(qk.astype(BF16), vh_bf) + w_inter * _dot(qh_bf, c_prev.astype(BF16))
        den = jnp.sum(qk, axis=1, keepdims=True) + w_inter * jnp.sum(qh * n_prev, axis=1, keepdims=True)
        hm = num / jnp.maximum(jnp.abs(den), jnp.exp(-s))
        hn = hm * lax.rsqrt(jnp.mean(hm * hm, axis=-1, keepdims=True) + EPS) * mnorm_ref[:, vsl]
        y_ref[:, vsl] = (hn * jax.nn.sigmoid(o_ref[:, vsl])).astype(y_ref.dtype)

        m_new = s[q - 1:q, :]
        wk = jnp.exp(b_last[:, h:h + 1] - bc + li_col[:, h:h + 1] - m_new)
        decay = jnp.exp(b_last[:, h:h + 1] + m_prev - m_new)
        kw = kh * wk
        c_ref[h] = decay * c_prev + _dot_lt(kw.astype(BF16), vh_bf)
        n_ref[h:h + 1, :] = decay * n_prev + jnp.sum(kw, axis=0, keepdims=True)
        m_ref[h:h + 1, :] = jnp.broadcast_to(m_new, (1, m_ref.shape[1]))


def _ml_prompt_specs(proj, n_chunks, n_batch, n_rows_total, p):
    q = CHUNK
    rb = lambda b, c: b * n_chunks + c

    def pspec(width, name):
        return pl.BlockSpec((q, width), lambda b, c: (rb(b, c), PROJ_SRC[name][1] // width))

    def wspec(shape):
        return pl.BlockSpec(shape, lambda b, c: (0,) * len(shape))

    return dict(
        in_specs=[pspec(1024, "q"), pspec(1024, "k"), pspec(2048, "v"), pspec(2048, "o"), pspec(SM_W, "s"),
                  wspec((1, 8)), wspec((8, 1)), wspec((1, 8)), wspec((8, 1)), wspec((1, 2048))],
        args=[*_proj_arrays(proj, "qkvos"),
              p["ib_row"], p["ib_col"], p["fb_row"], p["fb_col"], p["m_norm"]],
        out_specs=[pl.BlockSpec((q, D_MODEL), lambda b, c: (rb(b, c), 0)),
                   pl.BlockSpec((None, M_HEADS, M_DQK, M_DV), lambda b, c: (b, 0, 0, 0)),
                   pl.BlockSpec((None, M_HEADS, M_DQK), lambda b, c: (b, 0, 0)),
                   pl.BlockSpec((None, M_HEADS, 128), lambda b, c: (b, 0, 0))],
        out_shape=[jax.ShapeDtypeStruct((n_rows_total, D_MODEL), BF16),
                   jax.ShapeDtypeStruct((n_batch, M_HEADS, M_DQK, M_DV), F32),
                   jax.ShapeDtypeStruct((n_batch, M_HEADS, M_DQK), F32),
                   jax.ShapeDtypeStruct((n_batch, M_HEADS, 128), F32)])


def _ml_sample_kernel(q_ref, k_ref, v_ref, o_ref, sm_ref, ib_row_ref, fb_row_ref, mnorm_ref,
                      cin_ref, nin_ref, min_ref,
                      y_ref, cout_ref, nout_ref, mout_ref):
    r8 = SAMPLE_ROWS_PER_STEP
    li = sm_ref[:, S_IG:S_IG + M_HEADS] + ib_row_ref[...]
    lf = _log_sigmoid(sm_ref[:, S_FG:S_FG + M_HEADS] + fb_row_ref[...])
    m_prev = min_ref[...]
    inter = lf + m_prev
    s = jnp.maximum(inter, li)
    w_inter = jnp.exp(inter - s)
    w_in = jnp.exp(li - s)
    mout_ref[...] = s
    row_k = lax.broadcasted_iota(jnp.int32, (r8, M_DQK), 0)

    for h in range(M_HEADS):
        ksl = slice(h * M_DQK, (h + 1) * M_DQK)
        vsl = slice(h * M_DV, (h + 1) * M_DV)
        qh = q_ref[:, ksl]
        kh = k_ref[:, ksl] * (M_DQK ** -0.5)
        vh = v_ref[:, vsl]
        vh_bf = vh.astype(BF16)
        n_prev = nin_ref[:, ksl]
        wi = w_inter[:, h:h + 1]
        qk = jnp.sum(qh * kh, axis=1, keepdims=True) * w_in[:, h:h + 1]
        kw = kh * w_in[:, h:h + 1]
        qc = jnp.zeros((r8, M_DV), F32)
        for r in range(r8):
            c_prev = cin_ref[r, h]
            q_r = jnp.where(row_k == r, qh, 0.0).astype(BF16)
            qc = qc + _dot(q_r, c_prev.astype(BF16))
            kw_r = jnp.where(row_k == r, kw, 0.0).astype(BF16)
            cout_ref[r, h] = w_inter[r:r + 1, h:h + 1] * c_prev + _dot_lt(kw_r, vh_bf)
        nout_ref[:, ksl] = wi * n_prev + kw
        num = qk * vh + wi * qc
        den = qk + wi * jnp.sum(qh * n_prev, axis=1, keepdims=True)
        hm = num / jnp.maximum(jnp.abs(den), jnp.exp(-s[:, h:h + 1]))
        hn = hm * lax.rsqrt(jnp.mean(hm * hm, axis=-1, keepdims=True) + EPS) * mnorm_ref[:, vsl]
        y_ref[:, vsl] = hn * jax.nn.sigmoid(o_ref[:, vsl])


def _ml_sample(proj, row0, n_dec, layer, c_all, n_all, m_all, p, prev):
    r8 = SAMPLE_ROWS_PER_STEP
    rb0 = row0 // r8

    def pspec(width, name):
        return pl.BlockSpec((r8, width), lambda i: (rb0 + i, PROJ_SRC[name][1] // width))

    def wspec(shape):
        return pl.BlockSpec(shape, lambda i: (0,) * len(shape))

    cspec = pl.BlockSpec((None, r8, M_HEADS, M_DQK, M_DV), lambda i: (layer, i, 0, 0, 0))
    nspec = pl.BlockSpec((None, r8, M_HEADS * M_DQK), lambda i: (layer, i, 0))
    mspec = pl.BlockSpec((None, r8, M_HEADS), lambda i: (layer, i, 0))
    in_specs = [pspec(1024, "q"), pspec(1024, "k"), pspec(2048, "v"), pspec(2048, "o"), pspec(SM_W, "s"),
                wspec((1, 8)), wspec((1, 8)), wspec((1, 2048)), cspec, nspec, mspec]
    args = [*_proj_arrays(proj, "qkvos"), p["ib_row"], p["fb_row"], p["m_norm"], c_all, n_all, m_all]
    n_in = len(args)
    aliases = {}
    if prev is not None:
        in_specs += [pl.BlockSpec(memory_space=pl.ANY)] * 3
        args += list(prev)
        aliases = {n_in: 1, n_in + 1: 2, n_in + 2: 3}
    return pl.pallas_call(
        _drop_refs(_ml_sample_kernel, n_in, len(aliases)),
        grid=(n_dec // r8,),
        in_specs=in_specs,
        out_specs=[pl.BlockSpec((r8, D_MODEL), lambda i: (i, 0)), cspec, nspec, mspec],
        out_shape=[jax.ShapeDtypeStruct((n_dec, D_MODEL), F32),
                   jax.ShapeDtypeStruct(c_all.shape, F32),
                   jax.ShapeDtypeStruct(n_all.shape, F32),
                   jax.ShapeDtypeStruct(m_all.shape, F32)],
        input_output_aliases=aliases,
        compiler_params=_cp("parallel"),
        name="ml_sample",
    )(*args)


FFN_TN = 512
FFN_SUB = 256


def _ffn_gate_prompt_kernel(a_ref, wg_ref, wu_ref, cw_ref, cb_ref, act_ref, tail_ref, *gpads):
    seq = a_ref.shape[0]
    a = a_ref[...]
    for h, gpad in enumerate(gpads):
        sl = slice(h * FFN_SUB, (h + 1) * FFN_SUB)
        g = _dot(a, wg_ref[:, sl].astype(BF16))
        u = _dot(a, wu_ref[:, sl].astype(BF16))
        gpad[0:8, :] = jnp.zeros((8, FFN_SUB), F32)
        gpad[8:seq + 8, :] = g
        acc = gpad[6:seq + 6, :] * cw_ref[0:1, sl]
        acc = acc + gpad[7:seq + 7, :] * cw_ref[1:2, sl]
        acc = acc + g * cw_ref[2:3, sl]
        act_ref[:, sl] = (_silu(acc + cb_ref[:, sl]) * u).astype(act_ref.dtype)
        tail_ref[:, sl] = g[seq - 8:seq, :]


def _ffn_gate_prompt(h2, w_gate, w_up, layer, n_batch, seq, n_rows_total, p):
    tn = FFN_TN
    k = h2.shape[1]
    wspec = pl.BlockSpec((None, k, tn), lambda b, j: (layer, 0, j))
    return pl.pallas_call(
        _ffn_gate_prompt_kernel,
        grid=(n_batch, D_FF // tn),
        in_specs=[_lhs_spec(seq, k), wspec, wspec,
                  pl.BlockSpec((3, tn), lambda b, j: (0, j)),
                  pl.BlockSpec((1, tn), lambda b, j: (0, j))],
        out_specs=[pl.BlockSpec((seq, tn), lambda b, j: (b, j)),
                   pl.BlockSpec((None, 8, tn), lambda b, j: (b, 0, j))],
        out_shape=[jax.ShapeDtypeStruct((n_rows_total, D_FF), BF16),
                   jax.ShapeDtypeStruct((n_batch, 8, D_FF), F32)],
        scratch_shapes=[pltpu.VMEM((seq + 8, FFN_SUB), F32)] * (tn // FFN_SUB),
        compiler_params=_cp("parallel", "arbitrary"),
        name="ffn_gate_prompt",
    )(h2, w_gate, w_up, p["ffn_cw"], p["ffn_cb"])


def _ffn_gate_sample_kernel(a_ref, wg_ref, wu_ref, s0_ref, s1_ref, cw_ref, cb_ref, act_any, act_ref, g_ref):
    del act_any
    a = a_ref[...]
    g = _dot(a, wg_ref[...].astype(BF16))
    u = _dot(a, wu_ref[...].astype(BF16))
    acc = s0_ref[...] * cw_ref[0:1, :]
    acc = acc + s1_ref[...] * cw_ref[1:2, :]
    acc = acc + g * cw_ref[2:3, :]
    act_ref[...] = (_silu(acc + cb_ref[...]) * u).astype(act_ref.dtype)
    g_ref[...] = g


def _ffn_gate_sample(h2, w_gate, w_up, layer, row0, n_dec, conv_state_all, p, act):
    tn = FFN_TN
    k = h2.shape[1]
    rb = row0 // n_dec
    nct = D_FF // tn
    wspec = pl.BlockSpec((None, k, tn), lambda j: (layer, 0, j))
    return pl.pallas_call(
        _ffn_gate_sample_kernel,
        grid=(nct,),
        in_specs=[pl.BlockSpec((n_dec, k), lambda j: (rb, 0)), wspec, wspec,
                  pl.BlockSpec((None, n_dec, tn), lambda j: (layer, 0, j)),
                  pl.BlockSpec((None, n_dec, tn), lambda j: (layer, 0, nct + j)),
                  pl.BlockSpec((3, tn), lambda j: (0, j)),
                  pl.BlockSpec((1, tn), lambda j: (0, j)),
                  pl.BlockSpec(memory_space=pl.ANY)],
        out_specs=[pl.BlockSpec((n_dec, tn), lambda j: (rb, j)),
                   pl.BlockSpec((n_dec, tn), lambda j: (0, j))],
        out_shape=[jax.ShapeDtypeStruct(act.shape, act.dtype),
                   jax.ShapeDtypeStruct((n_dec, D_FF), F32)],
        input_output_aliases={7: 0},
        compiler_params=_cp("arbitrary"),
        name="ffn_gate_sample",
    )(h2, w_gate, w_up, conv_state_all, conv_state_all, p["ffn_cw"], p["ffn_cb"], act)


def _pick_tile(m, candidates):
    for t in candidates:
        if m % t == 0:
            return t
    raise ValueError(f"no row tile for {m}")


def _layer_params(l, ssd_conv_w, ssd_conv_b, ssd_dt_bias, ssd_a_log, ssd_d, ssd_norm, gm_norm, gm_ws, gm_bs,
                  m_i_bias, m_f_bias, m_norm, ffn_conv_w, ffn_conv_b):
    cw, cb = ssd_conv_w[l], ssd_conv_b[l]
    return dict(
        cwx=cw[:, 0:2048], cbx=cb[None, 0:2048], cwb=cw[:, 2048:2560], cbb=cb[None, 2048:2560],
        cwc=cw[:, 2560:3072], cbc=cb[None, 2560:3072],
        dtb_row=ssd_dt_bias[l][None, :], dtb_col=ssd_dt_bias[l][:, None],
        alog_row=ssd_a_log[l][None, :], alog_col=ssd_a_log[l][:, None],
        dfull=jnp.repeat(ssd_d[l], SSD_HEAD_DIM)[None, :], ssd_norm=ssd_norm[l][None, :],
        gm_norm=gm_norm[l][None, :], gm_ws=gm_ws[l], gm_bst=gm_bs[l].T,
        gm_w0=jnp.repeat(gm_ws[l][:, 0, 0], GM_GROUP_DIM)[None, :],
        gm_b0=jnp.repeat(gm_bs[l][:, 0], GM_GROUP_DIM)[None, :],
        ib_row=m_i_bias[l][None, :], ib_col=m_i_bias[l][:, None],
        fb_row=m_f_bias[l][None, :], fb_col=m_f_bias[l][:, None], m_norm=m_norm[l][None, :],
        ffn_cw=ffn_conv_w[l], ffn_cb=ffn_conv_b[l][None, :],
    )


def kernel(x_prompt, x_sample, state_ssd, state_ssd_conv, state_mlstm_c, state_mlstm_n, state_mlstm_m,
           state_ffn_conv, norm1, w_in, ssd_conv_w, ssd_conv_b, ssd_dt_bias, ssd_a_log, ssd_d, ssd_norm,
           gm_norm, gm_ws, gm_bs, m_i_bias, m_f_bias, m_norm, w_branch, w_out, norm2,
           w_gate, w_up, ffn_conv_w, ffn_conv_b, w_down, final_norm):
    n_batch, seq, d = x_prompt.shape
    n_dec = x_sample.shape[0]
    depth = w_in.shape[0]
    n_chunks = seq // CHUNK
    n_prompt = n_batch * seq
    n_rows = n_prompt + n_dec
    tm = _pick_tile(n_rows, (832, 640, 384, 128))
    tm_wide = _pick_tile(n_rows, (1664, 832, 640, 384, 128))
    tm_mid = _pick_tile(n_rows, (2080, 1664, 832, 640, 384, 128))
    tm_big = _pick_tile(n_rows, (4160, 2080, 1664, 832, 640, 384, 128))
    w_in_t = jnp.swapaxes(w_in, 1, 2)

    ssd_conv_all = state_ssd_conv.reshape(depth, n_dec, -1)
    ssd_h_all = state_ssd.reshape(depth, n_dec, D_MODEL, SSD_STATE)
    ml_n_all = state_mlstm_n.reshape(depth, n_dec, -1)
    ffn_conv_all = state_ffn_conv.reshape(depth, n_dec, -1)
    s_ssd = None
    s_ml = None

    def xbc(rows):
        c0 = PROJ_SRC["x"][1]
        return rows[..., c0:c0 + 3072]

    def tail_rows(a, n_tail):
        return jnp.stack([a[(b + 1) * seq - n_tail:(b + 1) * seq] for b in range(n_batch)])

    x = jnp.concatenate([x_prompt.reshape(n_prompt, d), x_sample.reshape(n_dec, d)], axis=0)
    outs = [[] for _ in range(13)]
    for l in range(depth):
        p = _layer_params(l, ssd_conv_w, ssd_conv_b, ssd_dt_bias, ssd_a_log, ssd_d, ssd_norm, gm_norm, gm_ws, gm_bs,
                          m_i_bias, m_f_bias, m_norm, ffn_conv_w, ffn_conv_b)
        h = _rmsnorm_rows(x, norm1[l], BF16, tm)
        proj = {"a": _matmul_t(h, w_in_t, l, tm_big, 256, *W_IN_A),
                "b": _matmul_t(h, w_in_t, l, tm_big, 256, *W_IN_B),
                "g": _matmul_t(h, w_in_t, l, tm_big, 256, *W_IN_G),
                "s": _matmul_small(h, w_in_t, l, tm_mid)}

        y0, p_ssd, y2, p_c, p_n, p_m = _prompt_scans(proj, n_batch, n_chunks, n_rows, p)
        y1 = _gm_prompt(proj, n_prompt, n_rows, p)

        ys0, s_ssd = _ssd_sample(proj, n_prompt, n_dec, l, ssd_conv_all, ssd_h_all, p, s_ssd)
        ys2, *s_ml = _ml_sample(proj, n_prompt, n_dec, l, state_mlstm_c, ml_n_all, state_mlstm_m, p, s_ml)
        y0, y1, y2, s_vn = _gm_sample_pack(proj, n_prompt, n_dec, p, ys0, ys2, y0, y1, y2)

        merged = _merge(y0, y1, y2, w_branch, l, proj, tm_wide, 256)
        x = _matmul(merged, w_out, l, tm_big, 256, residual=x)

        h2 = _rmsnorm_rows(x, norm2[l], BF16, tm)
        act, g_tail = _ffn_gate_prompt(h2, w_gate, w_up, l, n_batch, seq, n_rows, p)
        act, g_dec = _ffn_gate_sample(h2, w_gate, w_up, l, n_prompt, n_dec, ffn_conv_all, p, act)
        x = _matmul(act, w_down, l, tm_wide, 256, residual=x)

        outs[0].append(p_ssd.reshape(n_batch, SSD_HEADS, SSD_HEAD_DIM, SSD_STATE))
        outs[1].append(xbc(tail_rows(proj["a"], 3)))
        outs[2].append(p_c)
        outs[3].append(p_n)
        outs[4].append(p_m[:, :, 0])
        outs[5].append(g_tail[:, 6:8, :])
        outs[7].append(jnp.concatenate([state_ssd_conv[l][:, 1:], xbc(proj["a"][n_prompt:])[:, None, :]], axis=1))
        outs[11].append(jnp.concatenate([state_ffn_conv[l][:, 1:], g_dec[:, None, :]], axis=1))
        outs[12].append(s_vn.reshape(n_dec, 1, D_MODEL))

    tf = _pick_tile(n_prompt, (1024, 512, 256, 128))
    y_prompt = _rmsnorm_rows(x, final_norm, F32, tf, 0, n_prompt).reshape(n_batch, seq, d)
    y_sample = _rmsnorm_rows(x, final_norm, F32, n_dec, n_prompt // n_dec, n_dec).reshape(n_dec, 1, d)
    s_c, s_n, s_m = s_ml
    stacked = {6: s_ssd.reshape(depth, n_dec, SSD_HEADS, SSD_HEAD_DIM, SSD_STATE), 8: s_c,
               9: s_n.reshape(depth, n_dec, M_HEADS, M_DQK), 10: s_m}
    return (y_prompt, y_sample) + tuple(stacked[i] if i in stacked else jnp.stack(outs[i]) for i in range(13))
```

```python
import functools

import jax
import jax.numpy as jnp
from jax import lax
from jax.experimental import pallas as pl
from jax.experimental.pallas import tpu as pltpu

F32 = jnp.float32
BF16 = jnp.bfloat16

D_MODEL = 2048
DEPTH = 4
CHUNK = 128
EPS = 1e-6
SSD_HEADS = 32
SSD_HEAD_DIM = 64
SSD_GROUPS = 4
SSD_STATE = 128
SSD_GROUP_COLS = D_MODEL // SSD_GROUPS
GM_GROUPS = 8
GM_GROUP_DIM = 256
M_HEADS = 8
M_DV = 256
M_DQK = 128
D_FF = 5632

PROJ_SRC = {"z": ("a", 0), "x": ("a", 2048), "B": ("a", 4096), "C": ("a", 4608),
            "u": ("b", 0), "vg": ("b", 2048), "q": ("b", 4096), "k": ("b", 5120), "v": ("b", 6144), "o": ("b", 8192),
            "g": ("g", 0), "s": ("s", 0)}
W_IN_A = (0, 5120)
W_IN_B = (5152, 10240)
W_IN_G = (15408, 6144)
W_IN_DT_COL = 5120
W_IN_IF_COL = 15392
SM_W = 256
S_DT = 0
S_IG = 128
S_FG = 136

VMEM_LIMIT_BYTES = 56 * 1024 * 1024
SAMPLE_ROWS_PER_STEP = 8


def _cp(*sem):
    return pltpu.CompilerParams(dimension_semantics=sem, vmem_limit_bytes=VMEM_LIMIT_BYTES)


def _dot(a, b):
    return jnp.dot(a, b, preferred_element_type=F32)


def _dot_rt(a, b):
    return lax.dot_general(a, b, (((1,), (1,)), ((), ())), preferred_element_type=F32)


def _dot_lt(a, b):
    return lax.dot_general(a, b, (((0,), (0,)), ((), ())), preferred_element_type=F32)


def _split3(a):
    a1 = a.astype(BF16)
    r1 = a - a1.astype(F32)
    a2 = r1.astype(BF16)
    a3 = (r1 - a2.astype(F32)).astype(BF16)
    return a1, a2, a3


def _dot3_l(a_f32, b_bf16):
    a1, a2, a3 = _split3(a_f32)
    return (_dot(a1, b_bf16) + _dot(a2, b_bf16)) + _dot(a3, b_bf16)


def _dot3_r(a_bf16, b_f32):
    b1, b2, b3 = _split3(b_f32)
    return (_dot(a_bf16, b1) + _dot(a_bf16, b2)) + _dot(a_bf16, b3)


def _tri(n, lower=True):
    r = lax.broadcasted_iota(jnp.int32, (n, n), 0)
    c = lax.broadcasted_iota(jnp.int32, (n, n), 1)
    return (r >= c) if lower else (r <= c)


LOG2E = 1.4426950408889634


def _softplus(x):
    return jnp.maximum(x, 0.0) + jnp.log1p(jnp.exp(-jnp.abs(x)))


def _log_sigmoid(x):
    return -_softplus(-x)


def _silu(x):
    return x * jax.nn.sigmoid(x)


GELU_A = -2.0 * 0.7978845608028654 * LOG2E
GELU_B = GELU_A * 0.044715


def _gelu_tanh(x):
    return x / (1.0 + jnp.exp2(x * (GELU_A + GELU_B * (x * x))))


def _rms_kernel(x_ref, g_ref, o_ref):
    x = x_ref[...]
    y = x * lax.rsqrt(jnp.mean(x * x, axis=-1, keepdims=True) + EPS)
    o_ref[...] = (y * g_ref[...]).astype(o_ref.dtype)


def _rmsnorm_rows(x, g, out_dtype, tm, row_block0=0, n_rows=None):
    m_total, d = x.shape
    n_rows = m_total if n_rows is None else n_rows
    return pl.pallas_call(
        _rms_kernel,
        grid=(n_rows // tm,),
        in_specs=[pl.BlockSpec((tm, d), lambda i: (row_block0 + i, 0)),
                  pl.BlockSpec((1, d), lambda i: (0, 0))],
        out_specs=pl.BlockSpec((tm, d), lambda i: (i, 0)),
        out_shape=jax.ShapeDtypeStruct((n_rows, d), out_dtype),
        compiler_params=_cp("parallel"),
        name="rmsnorm_rows",
    )(x, g.reshape(1, d))


def _mm_kernel(has_res, *refs):
    a_ref, w_ref = refs[0], refs[1]
    acc = _dot(a_ref[...], w_ref[...].astype(BF16))
    if has_res:
        acc = refs[2][...] + acc
    o_ref = refs[-1]
    o_ref[...] = acc.astype(o_ref.dtype)


SINGLE_BUFFER_LHS_BYTES = 6 * 1024 * 1024


def _lhs_spec(tm, k):
    if tm * k * 2 >= SINGLE_BUFFER_LHS_BYTES:
        return pl.BlockSpec((tm, k), lambda i, j: (i, 0), pipeline_mode=pl.Buffered(1))
    return pl.BlockSpec((tm, k), lambda i, j: (i, 0))


def _matmul(a, w, layer, tm, tn, residual=None):
    m, k = a.shape
    n = w.shape[-1]
    in_specs = [_lhs_spec(tm, k),
                pl.BlockSpec((None, k, tn), lambda i, j: (layer, 0, j))]
    args = [a, w]
    if residual is not None:
        in_specs.append(pl.BlockSpec((tm, tn), lambda i, j: (i, j)))
        args.append(residual)
    return pl.pallas_call(
        functools.partial(_mm_kernel, residual is not None),
        grid=(m // tm, n // tn),
        in_specs=in_specs,
        out_specs=pl.BlockSpec((tm, tn), lambda i, j: (i, j)),
        out_shape=jax.ShapeDtypeStruct((m, n), F32),
        compiler_params=_cp("parallel", "arbitrary"),
        name="matmul_res" if residual is not None else "matmul",
    )(*args)


def _mm_t_kernel(a_ref, wt_ref, o_ref):
    o_ref[...] = _dot_rt(a_ref[...], wt_ref[0].astype(BF16))


def _wt_spec(layer, rows, k, row_of_step):
    def index_map(*idx):
        r = row_of_step(*idx)
        return (layer, r if isinstance(r, int) else pl.multiple_of(r, 8), 0)

    return pl.BlockSpec((pl.Element(1), pl.Element(rows), pl.Element(k)), index_map)


def _matmul_t(a, wt, layer, tm, tn, col0, n):
    m, k = a.shape
    return pl.pallas_call(
        _mm_t_kernel,
        grid=(m // tm, n // tn),
        in_specs=[_lhs_spec(tm, k),
                  _wt_spec(layer, tn, k, lambda i, j: col0 + tn * j)],
        out_specs=pl.BlockSpec((tm, tn), lambda i, j: (i, j)),
        out_shape=jax.ShapeDtypeStruct((m, n), F32),
        compiler_params=_cp("parallel", "arbitrary"),
        name="matmul_t",
    )(a, wt)


def _mm_t_small_kernel(a_ref, wt_ref, wdt_ref, wif_ref, o_ref, s_ref):
    o_ref[...] = _dot_rt(a_ref[...], wt_ref[0].astype(BF16))

    @pl.when(pl.program_id(1) == 0)
    def _():
        w_small = jnp.concatenate([wdt_ref[0], wif_ref[0]], axis=0).astype(BF16)
        r = _dot_rt(a_ref[...], w_small)
        s_ref[...] = jnp.zeros(s_ref.shape, F32)
        s_ref[:, S_DT:S_DT + SSD_HEADS] = r[:, :SSD_HEADS]
        s_ref[:, S_IG:S_IG + 2 * M_HEADS] = r[:, SSD_HEADS:]


def _matmul_t_with_small(a, wt, layer, tm, tn, col0, n):
    m, k = a.shape
    return pl.pallas_call(
        _mm_t_small_kernel,
        grid=(m // tm, n // tn),
        in_specs=[_lhs_spec(tm, k),
                  _wt_spec(layer, tn, k, lambda i, j: col0 + tn * j),
                  _wt_spec(layer, SSD_HEADS, k, lambda i, j: W_IN_DT_COL),
                  _wt_spec(layer, 2 * M_HEADS, k, lambda i, j: W_IN_IF_COL)],
        out_specs=[pl.BlockSpec((tm, tn), lambda i, j: (i, j)),
                   pl.BlockSpec((tm, SM_W), lambda i, j: (i, 0))],
        out_shape=[jax.ShapeDtypeStruct((m, n), F32), jax.ShapeDtypeStruct((m, SM_W), F32)],
        compiler_params=_cp("parallel", "arbitrary"),
        name="matmul_t_small",
    )(a, wt, wt, wt)


def _merge_kernel(y0_ref, y1_ref, y2_ref, w_ref, g0_ref, g1_ref, g2_ref, o_ref):
    acc = jax.nn.sigmoid(g0_ref[...]) * _dot(y0_ref[...], w_ref[0].astype(BF16))
    acc = acc + jax.nn.sigmoid(g1_ref[...]) * _dot(y1_ref[...], w_ref[1].astype(BF16))
    acc = acc + jax.nn.sigmoid(g2_ref[...]) * _dot(y2_ref[...], w_ref[2].astype(BF16))
    o_ref[...] = acc.astype(o_ref.dtype)


def _merge(y0, y1, y2, w_branch, layer, proj, tm, tn):
    m = y0.shape[0]
    gb = 0
    gstep = D_MODEL // tn
    yspec = _lhs_spec(tm, D_MODEL)

    def gspec(b):
        return pl.BlockSpec((tm, tn), lambda i, j: (i, gb + b * gstep + j))

    return pl.pallas_call(
        _merge_kernel,
        grid=(m // tm, D_MODEL // tn),
        in_specs=[yspec, yspec, yspec,
                  pl.BlockSpec((None, 3, D_MODEL, tn), lambda i, j: (layer, 0, 0, j)),
                  gspec(0), gspec(1), gspec(2)],
        out_specs=pl.BlockSpec((tm, tn), lambda i, j: (i, j)),
        out_shape=jax.ShapeDtypeStruct((m, D_MODEL), BF16),
        compiler_params=_cp("parallel", "arbitrary"),
        name="merge",
    )(y0, y1, y2, w_branch, proj["g"], proj["g"], proj["g"])


def _pair_select(lane_lo, col_a, col_b):
    return jnp.where(lane_lo, col_a, col_b)


def _ssd_prompt_kernel(z_ref, x_ref, b_ref, c_ref, sm_ref,
                       cwx_ref, cbx_ref, cwb_ref, cbb_ref, cwc_ref, cbc_ref,
                       dtb_row_ref, dtb_col_ref, alog_row_ref, alog_col_ref, dfull_ref, norm_ref,
                       y_ref, hout_ref,
                       xpad, bpad, cpad, ht_scr, ybuf, side_work=None):
    c = pl.program_id(1)
    nc = pl.num_programs(1)
    q = CHUNK

    @pl.when(c == 0)
    def _():
        xpad[0:8, :] = jnp.zeros((8, xpad.shape[1]), F32)
        bpad[0:8, :] = jnp.zeros((8, bpad.shape[1]), F32)
        cpad[0:8, :] = jnp.zeros((8, cpad.shape[1]), F32)
        ht_scr[...] = jnp.zeros(ht_scr.shape, F32)

    @pl.when(c > 0)
    def _():
        xpad[0:8, :] = xpad[q:q + 8, :]
        bpad[0:8, :] = bpad[q:q + 8, :]
        cpad[0:8, :] = cpad[q:q + 8, :]

    if side_work is not None:
        side_work()

    xpad[8:q + 8, :] = x_ref[...]
    bpad[8:q + 8, :] = b_ref[...]
    cpad[8:q + 8, :] = c_ref[...]

    def conv(pad, w_ref, bias_ref, lanes):
        acc = pad[5:q + 5, lanes] * w_ref[0:1, lanes]
        acc = acc + pad[6:q + 6, lanes] * w_ref[1:2, lanes]
        acc = acc + pad[7:q + 7, lanes] * w_ref[2:3, lanes]
        acc = acc + pad[8:q + 8, lanes] * w_ref[3:4, lanes]
        return _silu(acc + bias_ref[:, lanes])

    bm = conv(bpad, cwb_ref, cbb_ref, slice(None))
    cm = conv(cpad, cwc_ref, cbc_ref, slice(None))

    sm = sm_ref[...]
    sm_t = sm.T
    dt_col = _softplus(sm[:, S_DT:S_DT + SSD_HEADS] + dtb_row_ref[...])
    dt_row = _softplus(sm_t[S_DT:S_DT + SSD_HEADS, :] + dtb_col_ref[...])
    da_col = dt_col * (-jnp.exp(alog_row_ref[...]))
    da_row = dt_row * (-jnp.exp(alog_col_ref[...]))
    tril = _tri(q, True)
    s_col = _dot3_r(tril.astype(BF16), da_col)
    s_row = _dot3_l(da_row, _tri(q, False).astype(BF16))
    s2_col = s_col * LOG2E
    s2_row = s_row * LOG2E - jnp.log2(dt_row)
    es_col = jnp.exp(s_col)
    s_last = s_col[q - 1:q, :]
    wend_col = jnp.exp(s_last - s_col) * dt_col
    elast = jnp.exp(s_last)

    lane_lo = lax.broadcasted_iota(jnp.int32, (q, 128), 1) < SSD_HEAD_DIM
    lane_lo1 = lax.broadcasted_iota(jnp.int32, (1, 128), 1) < SSD_HEAD_DIM

    for g in range(SSD_GROUPS):
        bg = bm[:, g * SSD_STATE:(g + 1) * SSD_STATE]
        cg = cm[:, g * SSD_STATE:(g + 1) * SSD_STATE].astype(BF16)
        bg_t = bg.T.astype(BF16)
        cb = _dot(cg, bg_t)
        gsl = slice(g * SSD_GROUP_COLS, (g + 1) * SSD_GROUP_COLS)
        yint = _dot(cg, ht_scr[:, gsl].astype(BF16))
        for jj in range(4):
            j = g * 4 + jj
            ha, hb = 2 * j, 2 * j + 1
            psl = slice(j * 128, (j + 1) * 128)
            xp = conv(xpad, cwx_ref, cbx_ref, psl)
            xp_bf = xp.astype(BF16)
            ys = []
            for h in (ha, hb):
                dec_dt = jnp.exp2(jnp.where(tril, s2_col[:, h:h + 1] - s2_row[h:h + 1, :], -jnp.inf))
                ys.append(_dot((cb * dec_dt).astype(BF16), xp_bf))
            y = jnp.where(lane_lo, ys[0], ys[1])
            y = y + yint[:, jj * 128:(jj + 1) * 128] * _pair_select(lane_lo, es_col[:, ha:ha + 1], es_col[:, hb:hb + 1])
            y = y + dfull_ref[:, psl] * xp
            ybuf[:, psl] = y * _silu(z_ref[:, psl])
            xw = xp * _pair_select(lane_lo, wend_col[:, ha:ha + 1], wend_col[:, hb:hb + 1])
            dpair = _pair_select(lane_lo1, elast[:, ha:ha + 1], elast[:, hb:hb + 1])
            ht_scr[:, psl] = ht_scr[:, psl] * dpair + _dot(bg_t, xw.astype(BF16))
        yg = ybuf[:, gsl]
        yn = yg * lax.rsqrt(jnp.mean(yg * yg, axis=-1, keepdims=True) + EPS)
        y_ref[:, gsl] = (yn * norm_ref[:, gsl]).astype(y_ref.dtype)

    @pl.when(c == nc - 1)
    def _():
        hout_ref[...] = ht_scr[...].T


def _proj_arrays(proj, names):
    return [proj[PROJ_SRC[n][0]] for n in names]


N_SSD_IN, N_SSD_OUT, N_ML_IN, N_ML_OUT = 17, 2, 10, 4


def _ssd_ml_prompt_kernel(*refs):
    ssd_in = refs[:N_SSD_IN]
    ml_in = refs[N_SSD_IN:N_SSD_IN + N_ML_IN]
    pos = N_SSD_IN + N_ML_IN
    ssd_out = refs[pos:pos + N_SSD_OUT]
    ml_out = refs[pos + N_SSD_OUT:pos + N_SSD_OUT + N_ML_OUT]
    ssd_scratch = refs[pos + N_SSD_OUT + N_ML_OUT:]
    _ssd_prompt_kernel(*ssd_in, *ssd_out, *ssd_scratch,
                       side_work=functools.partial(_ml_prompt_kernel, *ml_in, *ml_out))


def _prompt_scans(proj, n_batch, n_chunks, n_rows_total, p):
    ssd = _ssd_prompt_specs(proj, n_chunks, n_batch, n_rows_total, p)
    ml = _ml_prompt_specs(proj, n_chunks, n_batch, n_rows_total, p)
    assert (len(ssd["args"]), len(ssd["out_shape"]), len(ml["args"]), len(ml["out_shape"])) == (
        N_SSD_IN, N_SSD_OUT, N_ML_IN, N_ML_OUT)
    return pl.pallas_call(
        _ssd_ml_prompt_kernel,
        grid=(n_batch, n_chunks),
        in_specs=ssd["in_specs"] + ml["in_specs"],
        out_specs=ssd["out_specs"] + ml["out_specs"],
        out_shape=ssd["out_shape"] + ml["out_shape"],
        scratch_shapes=ssd["scratch"],
        compiler_params=_cp("parallel", "arbitrary"),
        name="ssd_ml_prompt",
    )(*ssd["args"], *ml["args"])


def _ssd_prompt_specs(proj, n_chunks, n_batch, n_rows_total, p):
    q = CHUNK
    rb = lambda b, c: b * n_chunks + c

    def pspec(width, name):
        return pl.BlockSpec((q, width), lambda b, c: (rb(b, c), PROJ_SRC[name][1] // width))

    def wspec(shape):
        return pl.BlockSpec(shape, lambda b, c: (0,) * len(shape))

    return dict(
        in_specs=[pspec(2048, "z"), pspec(2048, "x"), pspec(512, "B"), pspec(512, "C"), pspec(SM_W, "s"),
                  wspec((4, 2048)), wspec((1, 2048)), wspec((4, 512)), wspec((1, 512)), wspec((4, 512)), wspec((1, 512)),
                  wspec((1, 32)), wspec((32, 1)), wspec((1, 32)), wspec((32, 1)), wspec((1, 2048)), wspec((1, 2048))],
        args=[*_proj_arrays(proj, "zxBCs"),
              p["cwx"], p["cbx"], p["cwb"], p["cbb"], p["cwc"], p["cbc"],
              p["dtb_row"], p["dtb_col"], p["alog_row"], p["alog_col"], p["dfull"], p["ssd_norm"]],
        out_specs=[pl.BlockSpec((q, D_MODEL), lambda b, c: (rb(b, c), 0)),
                   pl.BlockSpec((None, D_MODEL, SSD_STATE), lambda b, c: (b, 0, 0))],
        out_shape=[jax.ShapeDtypeStruct((n_rows_total, D_MODEL), BF16),
                   jax.ShapeDtypeStruct((n_batch, D_MODEL, SSD_STATE), F32)],
        scratch=[pltpu.VMEM((q + 8, 2048), F32), pltpu.VMEM((q + 8, 512), F32), pltpu.VMEM((q + 8, 512), F32),
                 pltpu.VMEM((SSD_STATE, D_MODEL), F32), pltpu.VMEM((q, D_MODEL), F32)])


def _ssd_sample_kernel(z_ref, x_ref, b_ref, c_ref, sm_ref, cs0_ref, cs1_ref, cs2_ref,
                       cwx_ref, cbx_ref, cwb_ref, cbb_ref, cwc_ref, cbc_ref,
                       dtb_row_ref, alog_row_ref, dfull_ref, norm_ref, hin_ref,
                       y_ref, hout_ref):
    r8 = SAMPLE_ROWS_PER_STEP

    def conv(lo, hi, new, w_ref, bias_ref):
        acc = cs0_ref[:, lo:hi] * w_ref[0:1, :]
        acc = acc + cs1_ref[:, lo:hi] * w_ref[1:2, :]
        acc = acc + cs2_ref[:, lo:hi] * w_ref[2:3, :]
        acc = acc + new * w_ref[3:4, :]
        return _silu(acc + bias_ref[...])

    xs = conv(0, 2048, x_ref[...], cwx_ref, cbx_ref)
    bm = conv(2048, 2560, b_ref[...], cwb_ref, cbb_ref)
    cm = conv(2560, 3072, c_ref[...], cwc_ref, cbc_ref)
    dt = _softplus(sm_ref[:, S_DT:S_DT + SSD_HEADS] + dtb_row_ref[...])
    e = jnp.exp(dt * (-jnp.exp(alog_row_ref[...])))
    hh = lax.broadcasted_iota(jnp.int32, (SSD_HEADS, D_MODEL), 0)
    cc = lax.broadcasted_iota(jnp.int32, (SSD_HEADS, D_MODEL), 1)
    expand = jnp.where((cc >= hh * SSD_HEAD_DIM) & (cc < (hh + 1) * SSD_HEAD_DIM), 1.0, 0.0).astype(BF16)
    dt_full = _dot3_l(dt, expand)
    dx = xs * dt_full
    row = lax.broadcasted_iota(jnp.int32, (r8, SSD_GROUP_COLS), 0)
    row_n = lax.broadcasted_iota(jnp.int32, (r8, SSD_STATE), 0)

    ygroups = []
    for g in range(SSD_GROUPS):
        gsl = slice(g * SSD_GROUP_COLS, (g + 1) * SSD_GROUP_COLS)
        nsl = slice(g * SSD_STATE, (g + 1) * SSD_STATE)
        bg = bm[:, nsl].astype(BF16)
        cg = cm[:, nsl]
        dxg = dx[:, gsl]
        yacc = jnp.zeros((r8, SSD_GROUP_COLS), F32)
        for r in range(r8):
            a_x = jnp.where(row == r, dxg, 0.0).astype(BF16)
            upd = _dot_lt(a_x, bg)
            hin = hin_ref[r, gsl, :]
            heads = []
            for hh in range(SSD_GROUP_COLS // SSD_HEAD_DIM):
                hrows = slice(hh * SSD_HEAD_DIM, (hh + 1) * SSD_HEAD_DIM)
                hcol = g * (SSD_GROUP_COLS // SSD_HEAD_DIM) + hh
                heads.append(hin[hrows, :] * e[r:r + 1, hcol:hcol + 1] + upd[hrows, :])
            hn = jnp.concatenate(heads, axis=0)
            hout_ref[r, gsl, :] = hn
            c_r = jnp.where(row_n == r, cg, 0.0).astype(BF16)
            yacc = yacc + _dot_rt(c_r, hn.astype(BF16))
        ygroups.append(yacc)

    for g in range(SSD_GROUPS):
        gsl = slice(g * SSD_GROUP_COLS, (g + 1) * SSD_GROUP_COLS)
        y = ygroups[g] + dfull_ref[:, gsl] * xs[:, gsl]
        y = y * _silu(z_ref[:, gsl])
        yn = y * lax.rsqrt(jnp.mean(y * y, axis=-1, keepdims=True) + EPS)
        y_ref[:, gsl] = yn * norm_ref[:, gsl]


def _drop_refs(kernel_fn, n_inputs, n_dropped):
    def body(*refs):
        return kernel_fn(*refs[:n_inputs], *refs[n_inputs + n_dropped:])
    return body


def _ssd_sample(proj, row0, n_dec, layer, conv_state_all, h_state_all, p, h_out_prev):
    r8 = SAMPLE_ROWS_PER_STEP
    rb0 = row0 // r8
    depth = h_state_all.shape[0]

    def pspec(width, name):
        return pl.BlockSpec((r8, width), lambda i: (rb0 + i, PROJ_SRC[name][1] // width))

    def wspec(shape):
        return pl.BlockSpec(shape, lambda i: (0,) * len(shape))

    def cspec(j):
        return pl.BlockSpec((None, r8, 3072), lambda i: (layer, i, j))

    hspec = pl.BlockSpec((None, r8, D_MODEL, SSD_STATE), lambda i: (layer, i, 0, 0))
    in_specs = [pspec(2048, "z"), pspec(2048, "x"), pspec(512, "B"), pspec(512, "C"), pspec(SM_W, "s"),
                cspec(0), cspec(1), cspec(2),
                wspec((4, 2048)), wspec((1, 2048)), wspec((4, 512)), wspec((1, 512)), wspec((4, 512)), wspec((1, 512)),
                wspec((1, 32)), wspec((1, 32)), wspec((1, 2048)), wspec((1, 2048)), hspec]
    args = [*_proj_arrays(proj, "zxBCs"), conv_state_all, conv_state_all, conv_state_all,
            p["cwx"], p["cbx"], p["cwb"], p["cbb"], p["cwc"], p["cbc"],
            p["dtb_row"], p["alog_row"], p["dfull"], p["ssd_norm"], h_state_all]
    n_in = len(args)
    aliases = {}
    if h_out_prev is not None:
        in_specs.append(pl.BlockSpec(memory_space=pl.ANY))
        args.append(h_out_prev)
        aliases = {n_in: 1}
    return pl.pallas_call(
        _drop_refs(_ssd_sample_kernel, n_in, len(aliases)),
        grid=(n_dec // r8,),
        in_specs=in_specs,
        out_specs=[pl.BlockSpec((r8, D_MODEL), lambda i: (i, 0)), hspec],
        out_shape=[jax.ShapeDtypeStruct((n_dec, D_MODEL), F32),
                   jax.ShapeDtypeStruct((depth, n_dec, D_MODEL, SSD_STATE), F32)],
        input_output_aliases=aliases,
        compiler_params=_cp("parallel"),
        name="ssd_sample",
    )(*args)


GM_ROWS_PER_STEP = 512


def _gm_prompt_kernel(u_ref, v_ref, gnorm_ref, ws_ref, bst_ref, y_ref):
    q = CHUNK
    tril = _tri(q, True)
    for cc in range(GM_ROWS_PER_STEP // q):
        rsl = slice(cc * q, (cc + 1) * q)
        u = _gelu_tanh(u_ref[rsl, :])
        v = _gelu_tanh(v_ref[rsl, :])
        vn = v * lax.rsqrt(jnp.mean(v * v, axis=-1, keepdims=True) + EPS) * gnorm_ref[...]
        for g in range(GM_GROUPS):
            gsl = slice(g * GM_GROUP_DIM, (g + 1) * GM_GROUP_DIM)
            w = jnp.where(tril, ws_ref[g], 0.0).astype(BF16)
            mixed = _dot(w, vn[:, gsl].astype(BF16)) + bst_ref[:, g:g + 1]
            y_ref[rsl, gsl] = (u[:, gsl] * mixed).astype(y_ref.dtype)


def _gm_prompt(proj, n_prompt_rows, n_rows_total, p):
    r = GM_ROWS_PER_STEP
    return pl.pallas_call(
        _gm_prompt_kernel,
        grid=(n_prompt_rows // r,),
        in_specs=[pl.BlockSpec((r, 2048), lambda i: (i, PROJ_SRC["u"][1] // 2048)),
                  pl.BlockSpec((r, 2048), lambda i: (i, PROJ_SRC["vg"][1] // 2048)),
                  pl.BlockSpec((1, 2048), lambda i: (0, 0)),
                  pl.BlockSpec((GM_GROUPS, CHUNK, CHUNK), lambda i: (0, 0, 0)),
                  pl.BlockSpec((CHUNK, GM_GROUPS), lambda i: (0, 0))],
        out_specs=pl.BlockSpec((r, D_MODEL), lambda i: (i, 0)),
        out_shape=jax.ShapeDtypeStruct((n_rows_total, D_MODEL), BF16),
        compiler_params=_cp("parallel"),
        name="gm_prompt",
    )(proj["b"], proj["b"], p["gm_norm"], p["gm_ws"], p["gm_bst"])


def _gm_sample_pack_kernel(u_ref, v_ref, gnorm_ref, w0_ref, b0_ref, yssd_ref, ym_ref,
                           y0_any, y1_any, y2_any,
                           o0_ref, o1_ref, o2_ref, vn_ref):
    del y0_any, y1_any, y2_any
    u = _gelu_tanh(u_ref[...])
    v = _gelu_tanh(v_ref[...])
    vn = v * lax.rsqrt(jnp.mean(v * v, axis=-1, keepdims=True) + EPS) * gnorm_ref[...]
    vn_ref[...] = vn
    mixed = w0_ref[...] * vn + b0_ref[...]
    o0_ref[...] = yssd_ref[...].astype(o0_ref.dtype)
    o1_ref[...] = (u * mixed).astype(o1_ref.dtype)
    o2_ref[...] = ym_ref[...].astype(o2_ref.dtype)


def _gm_sample_pack(proj, row0, n_dec, p, yssd_s, ym_s, y0, y1, y2):
    rb = row0 // n_dec
    full = lambda shape: pl.BlockSpec(shape, lambda i: (0,) * len(shape))
    anyspec = pl.BlockSpec(memory_space=pl.ANY)
    ospec = pl.BlockSpec((n_dec, D_MODEL), lambda i: (rb, 0))
    return pl.pallas_call(
        _gm_sample_pack_kernel,
        grid=(1,),
        in_specs=[pl.BlockSpec((n_dec, 2048), lambda i: (rb, PROJ_SRC["u"][1] // 2048)),
                  pl.BlockSpec((n_dec, 2048), lambda i: (rb, PROJ_SRC["vg"][1] // 2048)),
                  full((1, 2048)), full((1, 2048)), full((1, 2048)),
                  full((n_dec, D_MODEL)), full((n_dec, D_MODEL)),
                  anyspec, anyspec, anyspec],
        out_specs=[ospec, ospec, ospec, full((n_dec, D_MODEL))],
        out_shape=[jax.ShapeDtypeStruct(y0.shape, y0.dtype), jax.ShapeDtypeStruct(y1.shape, y1.dtype),
                   jax.ShapeDtypeStruct(y2.shape, y2.dtype), jax.ShapeDtypeStruct((n_dec, D_MODEL), F32)],
        input_output_aliases={7: 0, 8: 1, 9: 2},
        compiler_params=_cp("arbitrary"),
        name="gm_sample_pack",
    )(proj["b"], proj["b"], p["gm_norm"], p["gm_w0"], p["gm_b0"], yssd_s, ym_s, y0, y1, y2)


def _ml_prompt_kernel(q_ref, k_ref, v_ref, o_ref, sm_ref,
                      ib_row_ref, ib_col_ref, fb_row_ref, fb_col_ref, mnorm_ref,
                      y_ref, c_ref, n_ref, m_ref):
    c = pl.program_id(1)
    q = CHUNK

    @pl.when(c == 0)
    def _():
        c_ref[...] = jnp.zeros(c_ref.shape, F32)
        n_ref[...] = jnp.zeros(n_ref.shape, F32)
        m_ref[...] = jnp.zeros(m_ref.shape, F32)

    sm = sm_ref[...]
    sm_t = sm.T
    li_col = sm[:, S_IG:S_IG + M_HEADS] + ib_row_ref[...]
    lf_col = _log_sigmoid(sm[:, S_FG:S_FG + M_HEADS] + fb_row_ref[...])
    li_row = sm_t[S_IG:S_IG + M_HEADS, :] + ib_col_ref[...]
    lf_row = _log_sigmoid(sm_t[S_FG:S_FG + M_HEADS, :] + fb_col_ref[...])
    tril = _tri(q, True)
    b_col = _dot3_r(tril.astype(BF16), lf_col)
    b_row = _dot3_l(lf_row, _tri(q, False).astype(BF16))
    b_last = b_col[q - 1:q, :]

    for h in range(M_HEADS):
        ksl = slice(h * M_DQK, (h + 1) * M_DQK)
        vsl = slice(h * M_DV, (h + 1) * M_DV)
        m_prev = m_ref[h:h + 1, 0:1]
        bc = b_col[:, h:h + 1]
        log_d = jnp.where(tril, bc - b_row[h:h + 1, :] + li_row[h:h + 1, :], -jnp.inf)
        inter = bc + m_prev
        s = jnp.maximum(inter, jnp.max(log_d, axis=1, keepdims=True))
        w_inter = jnp.exp(inter - s)
        qh = q_ref[:, ksl]
        qh_bf = qh.astype(BF16)
        kh = k_ref[:, ksl] * (M_DQK ** -0.5)
        vh_bf = v_ref[:, vsl].astype(BF16)
        qk = _dot_rt(qh_bf, kh.astype(BF16)) * jnp.exp(log_d - s)
        c_prev = c_ref[h]
        n_prev = n_ref[h:h + 1, :]
        num = _dot(qk.astype(BF16), vh_bf) + w_inter * _dot(qh_bf, c_prev.astype(BF16))
        den = jnp.sum(qk, axis=1, keepdims=True) + w_inter * jnp.sum(qh * n_prev, axis=1, keepdims=True)
        hm = num / jnp.maximum(jnp.abs(den), jnp.exp(-s))
        hn = hm * lax.rsqrt(jnp.mean(hm * hm, axis=-1, keepdims=True) + EPS) * mnorm_ref[:, vsl]
        y_ref[:, vsl] = (hn * jax.nn.sigmoid(o_ref[:, vsl])).astype(y_ref.dtype)

        m_new = s[q - 1:q, :]
        wk = jnp.exp(b_last[:, h:h + 1] - bc + li_col[:, h:h + 1] - m_new)
        decay = jnp.exp(b_last[:, h:h + 1] + m_prev - m_new)
        kw = kh * wk
        c_ref[h] = decay * c_prev + _dot_lt(kw.astype(BF16), vh_bf)
        n_ref[h:h + 1, :] = decay * n_prev + jnp.sum(kw, axis=0, keepdims=True)
        m_ref[h:h + 1, :] = jnp.broadcast_to(m_new, (1, m_ref.shape[1]))


def _ml_prompt_specs(proj, n_chunks, n_batch, n_rows_total, p):
    q = CHUNK
    rb = lambda b, c: b * n_chunks + c

    def pspec(width, name):
        return pl.BlockSpec((q, width), lambda b, c: (rb(b, c), PROJ_SRC[name][1] // width))

    def wspec(shape):
        return pl.BlockSpec(shape, lambda b, c: (0,) * len(shape))

    return dict(
        in_specs=[pspec(1024, "q"), pspec(1024, "k"), pspec(2048, "v"), pspec(2048, "o"), pspec(SM_W, "s"),
                  wspec((1, 8)), wspec((8, 1)), wspec((1, 8)), wspec((8, 1)), wspec((1, 2048))],
        args=[*_proj_arrays(proj, "qkvos"),
              p["ib_row"], p["ib_col"], p["fb_row"], p["fb_col"], p["m_norm"]],
        out_specs=[pl.BlockSpec((q, D_MODEL), lambda b, c: (rb(b, c), 0)),
                   pl.BlockSpec((None, M_HEADS, M_DQK, M_DV), lambda b, c: (b, 0, 0, 0)),
                   pl.BlockSpec((None, M_HEADS, M_DQK), lambda b, c: (b, 0, 0)),
                   pl.BlockSpec((None, M_HEADS, 128), lambda b, c: (b, 0, 0))],
        out_shape=[jax.ShapeDtypeStruct((n_rows_total, D_MODEL), BF16),
                   jax.ShapeDtypeStruct((n_batch, M_HEADS, M_DQK, M_DV), F32),
                   jax.ShapeDtypeStruct((n_batch, M_HEADS, M_DQK), F32),
                   jax.ShapeDtypeStruct((n_batch, M_HEADS, 128), F32)])


def _ml_sample_kernel(q_ref, k_ref, v_ref, o_ref, sm_ref, ib_row_ref, fb_row_ref, mnorm_ref,
                      cin_ref, nin_ref, min_ref,
                      y_ref, cout_ref, nout_ref, mout_ref):
    r8 = SAMPLE_ROWS_PER_STEP
    li = sm_ref[:, S_IG:S_IG + M_HEADS] + ib_row_ref[...]
    lf = _log_sigmoid(sm_ref[:, S_FG:S_FG + M_HEADS] + fb_row_ref[...])
    m_prev = min_ref[...]
    inter = lf + m_prev
    s = jnp.maximum(inter, li)
    w_inter = jnp.exp(inter - s)
    w_in = jnp.exp(li - s)
    mout_ref[...] = s
    row_k = lax.broadcasted_iota(jnp.int32, (r8, M_DQK), 0)

    for h in range(M_HEADS):
        ksl = slice(h * M_DQK, (h + 1) * M_DQK)
        vsl = slice(h * M_DV, (h + 1) * M_DV)
        qh = q_ref[:, ksl]
        kh = k_ref[:, ksl] * (M_DQK ** -0.5)
        vh = v_ref[:, vsl]
        vh_bf = vh.astype(BF16)
        n_prev = nin_ref[:, ksl]
        wi = w_inter[:, h:h + 1]
        qk = jnp.sum(qh * kh, axis=1, keepdims=True) * w_in[:, h:h + 1]
        kw = kh * w_in[:, h:h + 1]
        qc = jnp.zeros((r8, M_DV), F32)
        for r in range(r8):
            c_prev = cin_ref[r, h]
            q_r = jnp.where(row_k == r, qh, 0.0).astype(BF16)
            qc = qc + _dot(q_r, c_prev.astype(BF16))
            kw_r = jnp.where(row_k == r, kw, 0.0).astype(BF16)
            cout_ref[r, h] = w_inter[r:r + 1, h:h + 1] * c_prev + _dot_lt(kw_r, vh_bf)
        nout_ref[:, ksl] = wi * n_prev + kw
        num = qk * vh + wi * qc
        den = qk + wi * jnp.sum(qh * n_prev, axis=1, keepdims=True)
        hm = num / jnp.maximum(jnp.abs(den), jnp.exp(-s[:, h:h + 1]))
        hn = hm * lax.rsqrt(jnp.mean(hm * hm, axis=-1, keepdims=True) + EPS) * mnorm_ref[:, vsl]
        y_ref[:, vsl] = hn * jax.nn.sigmoid(o_ref[:, vsl])


def _ml_sample(proj, row0, n_dec, layer, c_all, n_all, m_all, p, prev):
    r8 = SAMPLE_ROWS_PER_STEP
    rb0 = row0 // r8

    def pspec(width, name):
        return pl.BlockSpec((r8, width), lambda i: (rb0 + i, PROJ_SRC[name][1] // width))

    def wspec(shape):
        return pl.BlockSpec(shape, lambda i: (0,) * len(shape))

    cspec = pl.BlockSpec((None, r8, M_HEADS, M_DQK, M_DV), lambda i: (layer, i, 0, 0, 0))
    nspec = pl.BlockSpec((None, r8, M_HEADS * M_DQK), lambda i: (layer, i, 0))
    mspec = pl.BlockSpec((None, r8, M_HEADS), lambda i: (layer, i, 0))
    in_specs = [pspec(1024, "q"), pspec(1024, "k"), pspec(2048, "v"), pspec(2048, "o"), pspec(SM_W, "s"),
                wspec((1, 8)), wspec((1, 8)), wspec((1, 2048)), cspec, nspec, mspec]
    args = [*_proj_arrays(proj, "qkvos"), p["ib_row"], p["fb_row"], p["m_norm"], c_all, n_all, m_all]
    n_in = len(args)
    aliases = {}
    if prev is not None:
        in_specs += [pl.BlockSpec(memory_space=pl.ANY)] * 3
        args += list(prev)
        aliases = {n_in: 1, n_in + 1: 2, n_in + 2: 3}
    return pl.pallas_call(
        _drop_refs(_ml_sample_kernel, n_in, len(aliases)),
        grid=(n_dec // r8,),
        in_specs=in_specs,
        out_specs=[pl.BlockSpec((r8, D_MODEL), lambda i: (i, 0)), cspec, nspec, mspec],
        out_shape=[jax.ShapeDtypeStruct((n_dec, D_MODEL), F32),
                   jax.ShapeDtypeStruct(c_all.shape, F32),
                   jax.ShapeDtypeStruct(n_all.shape, F32),
                   jax.ShapeDtypeStruct(m_all.shape, F32)],
        input_output_aliases=aliases,
        compiler_params=_cp("parallel"),
        name="ml_sample",
    )(*args)


FFN_TN = 512
FFN_SUB = 256


def _ffn_gate_prompt_kernel(a_ref, wg_ref, wu_ref, cw_ref, cb_ref, act_ref, tail_ref, *gpads):
    seq = a_ref.shape[0]
    a = a_ref[...]
    for h, gpad in enumerate(gpads):
        sl = slice(h * FFN_SUB, (h + 1) * FFN_SUB)
        g = _dot(a, wg_ref[:, sl].astype(BF16))
        u = _dot(a, wu_ref[:, sl].astype(BF16))
        gpad[0:8, :] = jnp.zeros((8, FFN_SUB), F32)
        gpad[8:seq + 8, :] = g
        acc = gpad[6:seq + 6, :] * cw_ref[0:1, sl]
        acc = acc + gpad[7:seq + 7, :] * cw_ref[1:2, sl]
        acc = acc + g * cw_ref[2:3, sl]
        act_ref[:, sl] = (_silu(acc + cb_ref[:, sl]) * u).astype(act_ref.dtype)
        tail_ref[:, sl] = g[seq - 8:seq, :]


def _ffn_gate_prompt(h2, w_gate, w_up, layer, n_batch, seq, n_rows_total, p):
    tn = FFN_TN
    k = h2.shape[1]
    wspec = pl.BlockSpec((None, k, tn), lambda b, j: (layer, 0, j))
    return pl.pallas_call(
        _ffn_gate_prompt_kernel,
        grid=(n_batch, D_FF // tn),
        in_specs=[_lhs_spec(seq, k), wspec, wspec,
                  pl.BlockSpec((3, tn), lambda b, j: (0, j)),
                  pl.BlockSpec((1, tn), lambda b, j: (0, j))],
        out_specs=[pl.BlockSpec((seq, tn), lambda b, j: (b, j)),
                   pl.BlockSpec((None, 8, tn), lambda b, j: (b, 0, j))],
        out_shape=[jax.ShapeDtypeStruct((n_rows_total, D_FF), BF16),
                   jax.ShapeDtypeStruct((n_batch, 8, D_FF), F32)],
        scratch_shapes=[pltpu.VMEM((seq + 8, FFN_SUB), F32)] * (tn // FFN_SUB),
        compiler_params=_cp("parallel", "arbitrary"),
        name="ffn_gate_prompt",
    )(h2, w_gate, w_up, p["ffn_cw"], p["ffn_cb"])


def _ffn_gate_sample_kernel(a_ref, wg_ref, wu_ref, s0_ref, s1_ref, cw_ref, cb_ref, act_any, act_ref, g_ref):
    del act_any
    a = a_ref[...]
    g = _dot(a, wg_ref[...].astype(BF16))
    u = _dot(a, wu_ref[...].astype(BF16))
    acc = s0_ref[...] * cw_ref[0:1, :]
    acc = acc + s1_ref[...] * cw_ref[1:2, :]
    acc = acc + g * cw_ref[2:3, :]
    act_ref[...] = (_silu(acc + cb_ref[...]) * u).astype(act_ref.dtype)
    g_ref[...] = g


def _ffn_gate_sample(h2, w_gate, w_up, layer, row0, n_dec, conv_state_all, p, act):
    tn = FFN_TN
    k = h2.shape[1]
    rb = row0 // n_dec
    nct = D_FF // tn
    wspec = pl.BlockSpec((None, k, tn), lambda j: (layer, 0, j))
    return pl.pallas_call(
        _ffn_gate_sample_kernel,
        grid=(nct,),
        in_specs=[pl.BlockSpec((n_dec, k), lambda j: (rb, 0)), wspec, wspec,
                  pl.BlockSpec((None, n_dec, tn), lambda j: (layer, 0, j)),
                  pl.BlockSpec((None, n_dec, tn), lambda j: (layer, 0, nct + j)),
                  pl.BlockSpec((3, tn), lambda j: (0, j)),
                  pl.BlockSpec((1, tn), lambda j: (0, j)),
                  pl.BlockSpec(memory_space=pl.ANY)],
        out_specs=[pl.BlockSpec((n_dec, tn), lambda j: (rb, j)),
                   pl.BlockSpec((n_dec, tn), lambda j: (0, j))],
        out_shape=[jax.ShapeDtypeStruct(act.shape, act.dtype),
                   jax.ShapeDtypeStruct((n_dec, D_FF), F32)],
        input_output_aliases={7: 0},
        compiler_params=_cp("arbitrary"),
        name="ffn_gate_sample",
    )(h2, w_gate, w_up, conv_state_all, conv_state_all, p["ffn_cw"], p["ffn_cb"], act)


def _pick_tile(m, candidates):
    for t in candidates:
        if m % t == 0:
            return t
    raise ValueError(f"no row tile for {m}")


def _layer_params(l, ssd_conv_w, ssd_conv_b, ssd_dt_bias, ssd_a_log, ssd_d, ssd_norm, gm_norm, gm_ws, gm_bs,
                  m_i_bias, m_f_bias, m_norm, ffn_conv_w, ffn_conv_b):
    cw, cb = ssd_conv_w[l], ssd_conv_b[l]
    return dict(
        cwx=cw[:, 0:2048], cbx=cb[None, 0:2048], cwb=cw[:, 2048:2560], cbb=cb[None, 2048:2560],
        cwc=cw[:, 2560:3072], cbc=cb[None, 2560:3072],
        dtb_row=ssd_dt_bias[l][None, :], dtb_col=ssd_dt_bias[l][:, None],
        alog_row=ssd_a_log[l][None, :], alog_col=ssd_a_log[l][:, None],
        dfull=jnp.repeat(ssd_d[l], SSD_HEAD_DIM)[None, :], ssd_norm=ssd_norm[l][None, :],
        gm_norm=gm_norm[l][None, :], gm_ws=gm_ws[l], gm_bst=gm_bs[l].T,
        gm_w0=jnp.repeat(gm_ws[l][:, 0, 0], GM_GROUP_DIM)[None, :],
        gm_b0=jnp.repeat(gm_bs[l][:, 0], GM_GROUP_DIM)[None, :],
        ib_row=m_i_bias[l][None, :], ib_col=m_i_bias[l][:, None],
        fb_row=m_f_bias[l][None, :], fb_col=m_f_bias[l][:, None], m_norm=m_norm[l][None, :],
        ffn_cw=ffn_conv_w[l], ffn_cb=ffn_conv_b[l][None, :],
    )


def kernel(x_prompt, x_sample, state_ssd, state_ssd_conv, state_mlstm_c, state_mlstm_n, state_mlstm_m,
           state_ffn_conv, norm1, w_in, ssd_conv_w, ssd_conv_b, ssd_dt_bias, ssd_a_log, ssd_d, ssd_norm,
           gm_norm, gm_ws, gm_bs, m_i_bias, m_f_bias, m_norm, w_branch, w_out, norm2,
           w_gate, w_up, ffn_conv_w, ffn_conv_b, w_down, final_norm):
    n_batch, seq, d = x_prompt.shape
    n_dec = x_sample.shape[0]
    depth = w_in.shape[0]
    n_chunks = seq // CHUNK
    n_prompt = n_batch * seq
    n_rows = n_prompt + n_dec
    tm = _pick_tile(n_rows, (832, 640, 384, 128))
    tm_wide = _pick_tile(n_rows, (1664, 832, 640, 384, 128))
    tm_big = _pick_tile(n_rows, (4160, 2080, 1664, 832, 640, 384, 128))
    tm_half = _pick_tile(n_rows, (2080, 1664, 832, 640, 384, 128))
    w_in_t = jnp.swapaxes(w_in, 1, 2)

    ssd_conv_all = state_ssd_conv.reshape(depth, n_dec, -1)
    ssd_h_all = state_ssd.reshape(depth, n_dec, D_MODEL, SSD_STATE)
    ml_n_all = state_mlstm_n.reshape(depth, n_dec, -1)
    ffn_conv_all = state_ffn_conv.reshape(depth, n_dec, -1)
    s_ssd = None
    s_ml = None

    def xbc(rows):
        c0 = PROJ_SRC["x"][1]
        return rows[..., c0:c0 + 3072]

    def tail_rows(a, n_tail):
        return jnp.stack([a[(b + 1) * seq - n_tail:(b + 1) * seq] for b in range(n_batch)])

    x = jnp.concatenate([x_prompt.reshape(n_prompt, d), x_sample.reshape(n_dec, d)], axis=0)
    outs = [[] for _ in range(13)]
    for l in range(depth):
        p = _layer_params(l, ssd_conv_w, ssd_conv_b, ssd_dt_bias, ssd_a_log, ssd_d, ssd_norm, gm_norm, gm_ws, gm_bs,
                          m_i_bias, m_f_bias, m_norm, ffn_conv_w, ffn_conv_b)
        h = _rmsnorm_rows(x, norm1[l], BF16, tm)
        proj = {"b": _matmul_t(h, w_in_t, l, tm_big, 256, *W_IN_B),
                "g": _matmul_t(h, w_in_t, l, tm_big, 256, *W_IN_G)}
        proj["a"], proj["s"] = _matmul_t_with_small(h, w_in_t, l, tm_half, 256, *W_IN_A)

        y0, p_ssd, y2, p_c, p_n, p_m = _prompt_scans(proj, n_batch, n_chunks, n_rows, p)
        y1 = _gm_prompt(proj, n_prompt, n_rows, p)

        ys0, s_ssd = _ssd_sample(proj, n_prompt, n_dec, l, ssd_conv_all, ssd_h_all, p, s_ssd)
        ys2, *s_ml = _ml_sample(proj, n_prompt, n_dec, l, state_mlstm_c, ml_n_all, state_mlstm_m, p, s_ml)
        y0, y1, y2, s_vn = _gm_sample_pack(proj, n_prompt, n_dec, p, ys0, ys2, y0, y1, y2)

        merged = _merge(y0, y1, y2, w_branch, l, proj, tm_wide, 256)
        x = _matmul(merged, w_out, l, tm_big, 256, residual=x)

        h2 = _rmsnorm_rows(x, norm2[l], BF16, tm)
        act, g_tail = _ffn_gate_prompt(h2, w_gate, w_up, l, n_batch, seq, n_rows, p)
        act, g_dec = _ffn_gate_sample(h2, w_gate, w_up, l, n_prompt, n_dec, ffn_conv_all, p, act)
        x = _matmul(act, w_down, l, tm_wide, 256, residual=x)

        outs[0].append(p_ssd.reshape(n_batch, SSD_HEADS, SSD_HEAD_DIM, SSD_STATE))
        outs[1].append(xbc(tail_rows(proj["a"], 3)))
        outs[2].append(p_c)
        outs[3].append(p_n)
        outs[4].append(p_m[:, :, 0])
        outs[5].append(g_tail[:, 6:8, :])
        outs[7].append(jnp.concatenate([state_ssd_conv[l][:, 1:], xbc(proj["a"][n_prompt:])[:, None, :]], axis=1))
        outs[11].append(jnp.concatenate([state_ffn_conv[l][:, 1:], g_dec[:, None, :]], axis=1))
        outs[12].append(s_vn.reshape(n_dec, 1, D_MODEL))

    tf = _pick_tile(n_prompt, (1024, 512, 256, 128))
    y_prompt = _rmsnorm_rows(x, final_norm, F32, tf, 0, n_prompt).reshape(n_batch, seq, d)
    y_sample = _rmsnorm_rows(x, final_norm, F32, n_dec, n_prompt // n_dec, n_dec).reshape(n_dec, 1, d)
    s_c, s_n, s_m = s_ml
    stacked = {6: s_ssd.reshape(depth, n_dec, SSD_HEADS, SSD_HEAD_DIM, SSD_STATE), 8: s_c,
               9: s_n.reshape(depth, n_dec, M_HEADS, M_DQK), 10: s_m}
    return (y_prompt, y_sample) + tuple(stacked[i] if i in stacked else jnp.stack(outs[i]) for i in range(13))
```

```python
import functools

import jax
import jax.numpy as jnp
from jax import lax
from jax.experimental import pallas as pl
from jax.experimental.pallas import tpu as pltpu

F32 = jnp.float32
BF16 = jnp.bfloat16

D_MODEL = 2048
DEPTH = 4
CHUNK = 128
EPS = 1e-6
SSD_HEADS = 32
SSD_HEAD_DIM = 64
SSD_GROUPS = 4
SSD_STATE = 128
SSD_GROUP_COLS = D_MODEL // SSD_GROUPS
GM_GROUPS = 8
GM_GROUP_DIM = 256
M_HEADS = 8
M_DV = 256
M_DQK = 128
D_FF = 5632

PROJ_SRC = {"u": ("p", 0), "vg": ("p", 2048), "q": ("p", 4096), "k": ("p", 5120), "v": ("p", 6144), "o": ("p", 8192),
            "g": ("p", 10240),
            "z": ("p", 16384), "x": ("p", 18432), "B": ("p", 20480), "C": ("p", 20992),
            "s": ("s", 0)}
W_IN_A = (0, 5120)
W_IN_B = (5152, 10240)
W_IN_G = (15408, 6144)
W_IN_DT_COL = 5120
W_IN_IF_COL = 15392
SM_W = 256
S_DT = 0
S_IG = 128
S_FG = 136

VMEM_LIMIT_BYTES = 56 * 1024 * 1024
SAMPLE_ROWS_PER_STEP = 8


def _cp(*sem):
    return pltpu.CompilerParams(dimension_semantics=sem, vmem_limit_bytes=VMEM_LIMIT_BYTES)


def _dot(a, b):
    return jnp.dot(a, b, preferred_element_type=F32)


def _dot_rt(a, b):
    return lax.dot_general(a, b, (((1,), (1,)), ((), ())), preferred_element_type=F32)


def _dot_lt(a, b):
    return lax.dot_general(a, b, (((0,), (0,)), ((), ())), preferred_element_type=F32)


def _split3(a):
    a1 = a.astype(BF16)
    r1 = a - a1.astype(F32)
    a2 = r1.astype(BF16)
    a3 = (r1 - a2.astype(F32)).astype(BF16)
    return a1, a2, a3


def _dot3_l(a_f32, b_bf16):
    a1, a2, a3 = _split3(a_f32)
    return (_dot(a1, b_bf16) + _dot(a2, b_bf16)) + _dot(a3, b_bf16)


def _dot3_r(a_bf16, b_f32):
    b1, b2, b3 = _split3(b_f32)
    return (_dot(a_bf16, b1) + _dot(a_bf16, b2)) + _dot(a_bf16, b3)


def _tri(n, lower=True):
    r = lax.broadcasted_iota(jnp.int32, (n, n), 0)
    c = lax.broadcasted_iota(jnp.int32, (n, n), 1)
    return (r >= c) if lower else (r <= c)


LOG2E = 1.4426950408889634


def _softplus(x):
    return jnp.maximum(x, 0.0) + jnp.log1p(jnp.exp(-jnp.abs(x)))


def _log_sigmoid(x):
    return -_softplus(-x)


def _silu(x):
    return x * jax.nn.sigmoid(x)


GELU_A = -2.0 * 0.7978845608028654 * LOG2E
GELU_B = GELU_A * 0.044715


def _gelu_tanh(x):
    return x / (1.0 + jnp.exp2(x * (GELU_A + GELU_B * (x * x))))


def _rms_kernel(x_ref, g_ref, o_ref):
    x = x_ref[...]
    y = x * lax.rsqrt(jnp.mean(x * x, axis=-1, keepdims=True) + EPS)
    o_ref[...] = (y * g_ref[...]).astype(o_ref.dtype)


def _rmsnorm_rows(x, g, out_dtype, tm, row_block0=0, n_rows=None):
    m_total, d = x.shape
    n_rows = m_total if n_rows is None else n_rows
    return pl.pallas_call(
        _rms_kernel,
        grid=(n_rows // tm,),
        in_specs=[pl.BlockSpec((tm, d), lambda i: (row_block0 + i, 0)),
                  pl.BlockSpec((1, d), lambda i: (0, 0))],
        out_specs=pl.BlockSpec((tm, d), lambda i: (i, 0)),
        out_shape=jax.ShapeDtypeStruct((n_rows, d), out_dtype),
        compiler_params=_cp("parallel"),
        name="rmsnorm_rows",
    )(x, g.reshape(1, d))


def _mm_kernel(has_res, *refs):
    a_ref, w_ref = refs[0], refs[1]
    acc = _dot(a_ref[...], w_ref[...].astype(BF16))
    if has_res:
        acc = refs[2][...] + acc
    o_ref = refs[-1]
    o_ref[...] = acc.astype(o_ref.dtype)


SINGLE_BUFFER_LHS_BYTES = 6 * 1024 * 1024


def _lhs_spec(tm, k):
    if tm * k * 2 >= SINGLE_BUFFER_LHS_BYTES:
        return pl.BlockSpec((tm, k), lambda i, j: (i, 0), pipeline_mode=pl.Buffered(1))
    return pl.BlockSpec((tm, k), lambda i, j: (i, 0))


def _matmul(a, w, layer, tm, tn, residual=None):
    m, k = a.shape
    n = w.shape[-1]
    in_specs = [_lhs_spec(tm, k),
                pl.BlockSpec((None, k, tn), lambda i, j: (layer, 0, j))]
    args = [a, w]
    if residual is not None:
        in_specs.append(pl.BlockSpec((tm, tn), lambda i, j: (i, j)))
        args.append(residual)
    return pl.pallas_call(
        functools.partial(_mm_kernel, residual is not None),
        grid=(m // tm, n // tn),
        in_specs=in_specs,
        out_specs=pl.BlockSpec((tm, tn), lambda i, j: (i, j)),
        out_shape=jax.ShapeDtypeStruct((m, n), F32),
        compiler_params=_cp("parallel", "arbitrary"),
        name="matmul_res" if residual is not None else "matmul",
    )(*args)


def _mm_t_kernel(a_ref, wt_ref, o_ref):
    o_ref[...] = _dot_rt(a_ref[...], wt_ref[0].astype(BF16))


def _wt_spec(layer, rows, k, row_of_step):
    def index_map(*idx):
        r = row_of_step(*idx)
        return (layer, r if isinstance(r, int) else pl.multiple_of(r, 8), 0)

    return pl.BlockSpec((pl.Element(1), pl.Element(rows), pl.Element(k)), index_map)


def _matmul_t(a, wt, layer, tm, tn, segments):
    m, k = a.shape
    n_total = sum(n for _, n in segments)

    def source_row(i, j):
        row, first = None, 0
        for col0, n in segments:
            here = col0 + tn * (j - first)
            row = here if row is None else jnp.where(j >= first, here, row)
            first += n // tn
        return row

    return pl.pallas_call(
        _mm_t_kernel,
        grid=(m // tm, n_total // tn),
        in_specs=[_lhs_spec(tm, k), _wt_spec(layer, tn, k, source_row)],
        out_specs=pl.BlockSpec((tm, tn), lambda i, j: (i, j)),
        out_shape=jax.ShapeDtypeStruct((m, n_total), F32),
        compiler_params=_cp("parallel", "arbitrary"),
        name="matmul_t",
    )(a, wt)


def _mm_small_kernel(a_ref, wdt_ref, wif_ref, o_ref):
    a = a_ref[...]
    o_ref[...] = jnp.zeros(o_ref.shape, F32)
    o_ref[:, S_DT:S_DT + SSD_HEADS] = _dot_rt(a, wdt_ref[0].astype(BF16))
    o_ref[:, S_IG:S_IG + 2 * M_HEADS] = _dot_rt(a, wif_ref[0].astype(BF16))


def _matmul_small(a, wt, layer, tm):
    m, k = a.shape
    return pl.pallas_call(
        _mm_small_kernel,
        grid=(m // tm,),
        in_specs=[pl.BlockSpec((tm, k), lambda i: (i, 0)),
                  _wt_spec(layer, SSD_HEADS, k, lambda i: W_IN_DT_COL),
                  _wt_spec(layer, 2 * M_HEADS, k, lambda i: W_IN_IF_COL)],
        out_specs=pl.BlockSpec((tm, SM_W), lambda i: (i, 0)),
        out_shape=jax.ShapeDtypeStruct((m, SM_W), F32),
        compiler_params=_cp("parallel"),
        name="matmul_small",
    )(a, wt, wt)


def _merge_kernel(y0_ref, y1_ref, y2_ref, w_ref, g0_ref, g1_ref, g2_ref, o_ref):
    acc = jax.nn.sigmoid(g0_ref[...]) * _dot(y0_ref[...], w_ref[0].astype(BF16))
    acc = acc + jax.nn.sigmoid(g1_ref[...]) * _dot(y1_ref[...], w_ref[1].astype(BF16))
    acc = acc + jax.nn.sigmoid(g2_ref[...]) * _dot(y2_ref[...], w_ref[2].astype(BF16))
    o_ref[...] = acc.astype(o_ref.dtype)


def _merge(y0, y1, y2, w_branch, layer, proj, tm, tn):
    m = y0.shape[0]
    gb = PROJ_SRC["g"][1] // tn
    gstep = D_MODEL // tn
    yspec = _lhs_spec(tm, D_MODEL)

    def gspec(b):
        return pl.BlockSpec((tm, tn), lambda i, j: (i, gb + b * gstep + j))

    return pl.pallas_call(
        _merge_kernel,
        grid=(m // tm, D_MODEL // tn),
        in_specs=[yspec, yspec, yspec,
                  pl.BlockSpec((None, 3, D_MODEL, tn), lambda i, j: (layer, 0, 0, j)),
                  gspec(0), gspec(1), gspec(2)],
        out_specs=pl.BlockSpec((tm, tn), lambda i, j: (i, j)),
        out_shape=jax.ShapeDtypeStruct((m, D_MODEL), BF16),
        compiler_params=_cp("parallel", "arbitrary"),
        name="merge",
    )(y0, y1, y2, w_branch, *_proj_arrays(proj, "ggg"))


def _pair_select(lane_lo, col_a, col_b):
    return jnp.where(lane_lo, col_a, col_b)


def _ssd_prompt_kernel(z_ref, x_ref, b_ref, c_ref, sm_ref,
                       cwx_ref, cbx_ref, cwb_ref, cbb_ref, cwc_ref, cbc_ref,
                       dtb_row_ref, dtb_col_ref, alog_row_ref, alog_col_ref, dfull_ref, norm_ref,
                       y_ref, hout_ref,
                       xpad, bpad, cpad, ht_scr, ybuf, side_work=None):
    c = pl.program_id(1)
    nc = pl.num_programs(1)
    q = CHUNK

    @pl.when(c == 0)
    def _():
        xpad[0:8, :] = jnp.zeros((8, xpad.shape[1]), F32)
        bpad[0:8, :] = jnp.zeros((8, bpad.shape[1]), F32)
        cpad[0:8, :] = jnp.zeros((8, cpad.shape[1]), F32)
        ht_scr[...] = jnp.zeros(ht_scr.shape, F32)

    @pl.when(c > 0)
    def _():
        xpad[0:8, :] = xpad[q:q + 8, :]
        bpad[0:8, :] = bpad[q:q + 8, :]
        cpad[0:8, :] = cpad[q:q + 8, :]

    if side_work is not None:
        side_work()

    xpad[8:q + 8, :] = x_ref[...]
    bpad[8:q + 8, :] = b_ref[...]
    cpad[8:q + 8, :] = c_ref[...]

    def conv(pad, w_ref, bias_ref, lanes):
        acc = pad[5:q + 5, lanes] * w_ref[0:1, lanes]
        acc = acc + pad[6:q + 6, lanes] * w_ref[1:2, lanes]
        acc = acc + pad[7:q + 7, lanes] * w_ref[2:3, lanes]
        acc = acc + pad[8:q + 8, lanes] * w_ref[3:4, lanes]
        return _silu(acc + bias_ref[:, lanes])

    bm = conv(bpad, cwb_ref, cbb_ref, slice(None))
    cm = conv(cpad, cwc_ref, cbc_ref, slice(None))

    sm = sm_ref[...]
    sm_t = sm.T
    dt_col = _softplus(sm[:, S_DT:S_DT + SSD_HEADS] + dtb_row_ref[...])
    dt_row = _softplus(sm_t[S_DT:S_DT + SSD_HEADS, :] + dtb_col_ref[...])
    da_col = dt_col * (-jnp.exp(alog_row_ref[...]))
    da_row = dt_row * (-jnp.exp(alog_col_ref[...]))
    tril = _tri(q, True)
    s_col = _dot3_r(tril.astype(BF16), da_col)
    s_row = _dot3_l(da_row, _tri(q, False).astype(BF16))
    s2_col = s_col * LOG2E
    s2_row = s_row * LOG2E - jnp.log2(dt_row)
    es_col = jnp.exp(s_col)
    s_last = s_col[q - 1:q, :]
    wend_col = jnp.exp(s_last - s_col) * dt_col
    elast = jnp.exp(s_last)

    lane_lo = lax.broadcasted_iota(jnp.int32, (q, 128), 1) < SSD_HEAD_DIM
    lane_lo1 = lax.broadcasted_iota(jnp.int32, (1, 128), 1) < SSD_HEAD_DIM

    for g in range(SSD_GROUPS):
        bg = bm[:, g * SSD_STATE:(g + 1) * SSD_STATE]
        cg = cm[:, g * SSD_STATE:(g + 1) * SSD_STATE].astype(BF16)
        bg_t = bg.T.astype(BF16)
        cb = _dot(cg, bg_t)
        gsl = slice(g * SSD_GROUP_COLS, (g + 1) * SSD_GROUP_COLS)
        yint = _dot(cg, ht_scr[:, gsl].astype(BF16))
        for jj in range(4):
            j = g * 4 + jj
            ha, hb = 2 * j, 2 * j + 1
            psl = slice(j * 128, (j + 1) * 128)
            xp = conv(xpad, cwx_ref, cbx_ref, psl)
            xp_bf = xp.astype(BF16)
            ys = []
            for h in (ha, hb):
                dec_dt = jnp.exp2(jnp.where(tril, s2_col[:, h:h + 1] - s2_row[h:h + 1, :], -jnp.inf))
                ys.append(_dot((cb * dec_dt).astype(BF16), xp_bf))
            y = jnp.where(lane_lo, ys[0], ys[1])
            y = y + yint[:, jj * 128:(jj + 1) * 128] * _pair_select(lane_lo, es_col[:, ha:ha + 1], es_col[:, hb:hb + 1])
            y = y + dfull_ref[:, psl] * xp
            ybuf[:, psl] = y * _silu(z_ref[:, psl])
            xw = xp * _pair_select(lane_lo, wend_col[:, ha:ha + 1], wend_col[:, hb:hb + 1])
            dpair = _pair_select(lane_lo1, elast[:, ha:ha + 1], elast[:, hb:hb + 1])
            ht_scr[:, psl] = ht_scr[:, psl] * dpair + _dot(bg_t, xw.astype(BF16))
        yg = ybuf[:, gsl]
        yn = yg * lax.rsqrt(jnp.mean(yg * yg, axis=-1, keepdims=True) + EPS)
        y_ref[:, gsl] = (yn * norm_ref[:, gsl]).astype(y_ref.dtype)

    @pl.when(c == nc - 1)
    def _():
        hout_ref[...] = ht_scr[...].T


def _proj_arrays(proj, names):
    return [proj[PROJ_SRC[n][0]] for n in names]


N_SSD_IN, N_SSD_OUT, N_ML_IN, N_ML_OUT = 17, 2, 10, 4


def _ssd_ml_prompt_kernel(*refs):
    ssd_in = refs[:N_SSD_IN]
    ml_in = refs[N_SSD_IN:N_SSD_IN + N_ML_IN]
    pos = N_SSD_IN + N_ML_IN
    ssd_out = refs[pos:pos + N_SSD_OUT]
    ml_out = refs[pos + N_SSD_OUT:pos + N_SSD_OUT + N_ML_OUT]
    ssd_scratch = refs[pos + N_SSD_OUT + N_ML_OUT:]
    _ssd_prompt_kernel(*ssd_in, *ssd_out, *ssd_scratch,
                       side_work=functools.partial(_ml_prompt_kernel, *ml_in, *ml_out))


def _prompt_scans(proj, n_batch, n_chunks, n_rows_total, p):
    ssd = _ssd_prompt_specs(proj, n_chunks, n_batch, n_rows_total, p)
    ml = _ml_prompt_specs(proj, n_chunks, n_batch, n_rows_total, p)
    assert (len(ssd["args"]), len(ssd["out_shape"]), len(ml["args"]), len(ml["out_shape"])) == (
        N_SSD_IN, N_SSD_OUT, N_ML_IN, N_ML_OUT)
    return pl.pallas_call(
        _ssd_ml_prompt_kernel,
        grid=(n_batch, n_chunks),
        in_specs=ssd["in_specs"] + ml["in_specs"],
        out_specs=ssd["out_specs"] + ml["out_specs"],
        out_shape=ssd["out_shape"] + ml["out_shape"],
        scratch_shapes=ssd["scratch"],
        compiler_params=_cp("parallel", "arbitrary"),
        name="ssd_ml_prompt",
    )(*ssd["args"], *ml["args"])


def _ssd_prompt_specs(proj, n_chunks, n_batch, n_rows_total, p):
    q = CHUNK
    rb = lambda b, c: b * n_chunks + c

    def pspec(width, name):
        return pl.BlockSpec((q, width), lambda b, c: (rb(b, c), PROJ_SRC[name][1] // width))

    def wspec(shape):
        return pl.BlockSpec(shape, lambda b, c: (0,) * len(shape))

    return dict(
        in_specs=[pspec(2048, "z"), pspec(2048, "x"), pspec(512, "B"), pspec(512, "C"), pspec(SM_W, "s"),
                  wspec((4, 2048)), wspec((1, 2048)), wspec((4, 512)), wspec((1, 512)), wspec((4, 512)), wspec((1, 512)),
                  wspec((1, 32)), wspec((32, 1)), wspec((1, 32)), wspec((32, 1)), wspec((1, 2048)), wspec((1, 2048))],
        args=[*_proj_arrays(proj, "zxBCs"),
              p["cwx"], p["cbx"], p["cwb"], p["cbb"], p["cwc"], p["cbc"],
              p["dtb_row"], p["dtb_col"], p["alog_row"], p["alog_col"], p["dfull"], p["ssd_norm"]],
        out_specs=[pl.BlockSpec((q, D_MODEL), lambda b, c: (rb(b, c), 0)),
                   pl.BlockSpec((None, D_MODEL, SSD_STATE), lambda b, c: (b, 0, 0))],
        out_shape=[jax.ShapeDtypeStruct((n_rows_total, D_MODEL), BF16),
                   jax.ShapeDtypeStruct((n_batch, D_MODEL, SSD_STATE), F32)],
        scratch=[pltpu.VMEM((q + 8, 2048), F32), pltpu.VMEM((q + 8, 512), F32), pltpu.VMEM((q + 8, 512), F32),
                 pltpu.VMEM((SSD_STATE, D_MODEL), F32), pltpu.VMEM((q, D_MODEL), F32)])


def _ssd_sample_kernel(z_ref, x_ref, b_ref, c_ref, sm_ref, cs0_ref, cs1_ref, cs2_ref,
                       cwx_ref, cbx_ref, cwb_ref, cbb_ref, cwc_ref, cbc_ref,
                       dtb_row_ref, alog_row_ref, dfull_ref, norm_ref, hin_ref,
                       y_ref, hout_ref):
    r8 = SAMPLE_ROWS_PER_STEP

    def conv(lo, hi, new, w_ref, bias_ref):
        acc = cs0_ref[:, lo:hi] * w_ref[0:1, :]
        acc = acc + cs1_ref[:, lo:hi] * w_ref[1:2, :]
        acc = acc + cs2_ref[:, lo:hi] * w_ref[2:3, :]
        acc = acc + new * w_ref[3:4, :]
        return _silu(acc + bias_ref[...])

    xs = conv(0, 2048, x_ref[...], cwx_ref, cbx_ref)
    bm = conv(2048, 2560, b_ref[...], cwb_ref, cbb_ref)
    cm = conv(2560, 3072, c_ref[...], cwc_ref, cbc_ref)
    dt = _softplus(sm_ref[:, S_DT:S_DT + SSD_HEADS] + dtb_row_ref[...])
    e = jnp.exp(dt * (-jnp.exp(alog_row_ref[...])))
    hh = lax.broadcasted_iota(jnp.int32, (SSD_HEADS, D_MODEL), 0)
    cc = lax.broadcasted_iota(jnp.int32, (SSD_HEADS, D_MODEL), 1)
    expand = jnp.where((cc >= hh * SSD_HEAD_DIM) & (cc < (hh + 1) * SSD_HEAD_DIM), 1.0, 0.0).astype(BF16)
    dt_full = _dot3_l(dt, expand)
    dx = xs * dt_full
    row = lax.broadcasted_iota(jnp.int32, (r8, SSD_GROUP_COLS), 0)
    row_n = lax.broadcasted_iota(jnp.int32, (r8, SSD_STATE), 0)

    ygroups = []
    for g in range(SSD_GROUPS):
        gsl = slice(g * SSD_GROUP_COLS, (g + 1) * SSD_GROUP_COLS)
        nsl = slice(g * SSD_STATE, (g + 1) * SSD_STATE)
        bg = bm[:, nsl].astype(BF16)
        cg = cm[:, nsl]
        dxg = dx[:, gsl]
        yacc = jnp.zeros((r8, SSD_GROUP_COLS), F32)
        for r in range(r8):
            a_x = jnp.where(row == r, dxg, 0.0).astype(BF16)
            upd = _dot_lt(a_x, bg)
            hin = hin_ref[r, gsl, :]
            heads = []
            for hh in range(SSD_GROUP_COLS // SSD_HEAD_DIM):
                hrows = slice(hh * SSD_HEAD_DIM, (hh + 1) * SSD_HEAD_DIM)
                hcol = g * (SSD_GROUP_COLS // SSD_HEAD_DIM) + hh
                heads.append(hin[hrows, :] * e[r:r + 1, hcol:hcol + 1] + upd[hrows, :])
            hn = jnp.concatenate(heads, axis=0)
            hout_ref[r, gsl, :] = hn
            c_r = jnp.where(row_n == r, cg, 0.0).astype(BF16)
            yacc = yacc + _dot_rt(c_r, hn.astype(BF16))
        ygroups.append(yacc)

    for g in range(SSD_GROUPS):
        gsl = slice(g * SSD_GROUP_COLS, (g + 1) * SSD_GROUP_COLS)
        y = ygroups[g] + dfull_ref[:, gsl] * xs[:, gsl]
        y = y * _silu(z_ref[:, gsl])
        yn = y * lax.rsqrt(jnp.mean(y * y, axis=-1, keepdims=True) + EPS)
        y_ref[:, gsl] = yn * norm_ref[:, gsl]


def _drop_refs(kernel_fn, n_inputs, n_dropped):
    def body(*refs):
        return kernel_fn(*refs[:n_inputs], *refs[n_inputs + n_dropped:])
    return body


def _ssd_sample(proj, row0, n_dec, layer, conv_state_all, h_state_all, p, h_out_prev):
    r8 = SAMPLE_ROWS_PER_STEP
    rb0 = row0 // r8
    depth = h_state_all.shape[0]

    def pspec(width, name):
        return pl.BlockSpec((r8, width), lambda i: (rb0 + i, PROJ_SRC[name][1] // width))

    def wspec(shape):
        return pl.BlockSpec(shape, lambda i: (0,) * len(shape))

    def cspec(j):
        return pl.BlockSpec((None, r8, 3072), lambda i: (layer, i, j))

    hspec = pl.BlockSpec((None, r8, D_MODEL, SSD_STATE), lambda i: (layer, i, 0, 0))
    in_specs = [pspec(2048, "z"), pspec(2048, "x"), pspec(512, "B"), pspec(512, "C"), pspec(SM_W, "s"),
                cspec(0), cspec(1), cspec(2),
                wspec((4, 2048)), wspec((1, 2048)), wspec((4, 512)), wspec((1, 512)), wspec((4, 512)), wspec((1, 512)),
                wspec((1, 32)), wspec((1, 32)), wspec((1, 2048)), wspec((1, 2048)), hspec]
    args = [*_proj_arrays(proj, "zxBCs"), conv_state_all, conv_state_all, conv_state_all,
            p["cwx"], p["cbx"], p["cwb"], p["cbb"], p["cwc"], p["cbc"],
            p["dtb_row"], p["alog_row"], p["dfull"], p["ssd_norm"], h_state_all]
    n_in = len(args)
    aliases = {}
    if h_out_prev is not None:
        in_specs.append(pl.BlockSpec(memory_space=pl.ANY))
        args.append(h_out_prev)
        aliases = {n_in: 1}
    return pl.pallas_call(
        _drop_refs(_ssd_sample_kernel, n_in, len(aliases)),
        grid=(n_dec // r8,),
        in_specs=in_specs,
        out_specs=[pl.BlockSpec((r8, D_MODEL), lambda i: (i, 0)), hspec],
        out_shape=[jax.ShapeDtypeStruct((n_dec, D_MODEL), F32),
                   jax.ShapeDtypeStruct((depth, n_dec, D_MODEL, SSD_STATE), F32)],
        input_output_aliases=aliases,
        compiler_params=_cp("parallel"),
        name="ssd_sample",
    )(*args)


GM_ROWS_PER_STEP = 512


def _gm_prompt_kernel(u_ref, v_ref, gnorm_ref, ws_ref, bst_ref, y_ref):
    q = CHUNK
    tril = _tri(q, True)
    for cc in range(GM_ROWS_PER_STEP // q):
        rsl = slice(cc * q, (cc + 1) * q)
        u = _gelu_tanh(u_ref[rsl, :])
        v = _gelu_tanh(v_ref[rsl, :])
        vn = v * lax.rsqrt(jnp.mean(v * v, axis=-1, keepdims=True) + EPS) * gnorm_ref[...]
        for g in range(GM_GROUPS):
            gsl = slice(g * GM_GROUP_DIM, (g + 1) * GM_GROUP_DIM)
            w = jnp.where(tril, ws_ref[g], 0.0).astype(BF16)
            mixed = _dot(w, vn[:, gsl].astype(BF16)) + bst_ref[:, g:g + 1]
            y_ref[rsl, gsl] = (u[:, gsl] * mixed).astype(y_ref.dtype)


def _gm_prompt(proj, n_prompt_rows, n_rows_total, p):
    r = GM_ROWS_PER_STEP
    return pl.pallas_call(
        _gm_prompt_kernel,
        grid=(n_prompt_rows // r,),
        in_specs=[pl.BlockSpec((r, 2048), lambda i: (i, PROJ_SRC["u"][1] // 2048)),
                  pl.BlockSpec((r, 2048), lambda i: (i, PROJ_SRC["vg"][1] // 2048)),
                  pl.BlockSpec((1, 2048), lambda i: (0, 0)),
                  pl.BlockSpec((GM_GROUPS, CHUNK, CHUNK), lambda i: (0, 0, 0)),
                  pl.BlockSpec((CHUNK, GM_GROUPS), lambda i: (0, 0))],
        out_specs=pl.BlockSpec((r, D_MODEL), lambda i: (i, 0)),
        out_shape=jax.ShapeDtypeStruct((n_rows_total, D_MODEL), BF16),
        compiler_params=_cp("parallel"),
        name="gm_prompt",
    )(*_proj_arrays(proj, "uu"), p["gm_norm"], p["gm_ws"], p["gm_bst"])


def _gm_sample_pack_kernel(u_ref, v_ref, gnorm_ref, w0_ref, b0_ref, yssd_ref, ym_ref,
                           y0_any, y1_any, y2_any,
                           o0_ref, o1_ref, o2_ref, vn_ref):
    del y0_any, y1_any, y2_any
    u = _gelu_tanh(u_ref[...])
    v = _gelu_tanh(v_ref[...])
    vn = v * lax.rsqrt(jnp.mean(v * v, axis=-1, keepdims=True) + EPS) * gnorm_ref[...]
    vn_ref[...] = vn
    mixed = w0_ref[...] * vn + b0_ref[...]
    o0_ref[...] = yssd_ref[...].astype(o0_ref.dtype)
    o1_ref[...] = (u * mixed).astype(o1_ref.dtype)
    o2_ref[...] = ym_ref[...].astype(o2_ref.dtype)


def _gm_sample_pack(proj, row0, n_dec, p, yssd_s, ym_s, y0, y1, y2):
    rb = row0 // n_dec
    full = lambda shape: pl.BlockSpec(shape, lambda i: (0,) * len(shape))
    anyspec = pl.BlockSpec(memory_space=pl.ANY)
    ospec = pl.BlockSpec((n_dec, D_MODEL), lambda i: (rb, 0))
    return pl.pallas_call(
        _gm_sample_pack_kernel,
        grid=(1,),
        in_specs=[pl.BlockSpec((n_dec, 2048), lambda i: (rb, PROJ_SRC["u"][1] // 2048)),
                  pl.BlockSpec((n_dec, 2048), lambda i: (rb, PROJ_SRC["vg"][1] // 2048)),
                  full((1, 2048)), full((1, 2048)), full((1, 2048)),
                  full((n_dec, D_MODEL)), full((n_dec, D_MODEL)),
                  anyspec, anyspec, anyspec],
        out_specs=[ospec, ospec, ospec, full((n_dec, D_MODEL))],
        out_shape=[jax.ShapeDtypeStruct(y0.shape, y0.dtype), jax.ShapeDtypeStruct(y1.shape, y1.dtype),
                   jax.ShapeDtypeStruct(y2.shape, y2.dtype), jax.ShapeDtypeStruct((n_dec, D_MODEL), F32)],
        input_output_aliases={7: 0, 8: 1, 9: 2},
        compiler_params=_cp("arbitrary"),
        name="gm_sample_pack",
    )(*_proj_arrays(proj, "uu"), p["gm_norm"], p["gm_w0"], p["gm_b0"], yssd_s, ym_s, y0, y1, y2)


def _ml_prompt_kernel(q_ref, k_ref, v_ref, o_ref, sm_ref,
                      ib_row_ref, ib_col_ref, fb_row_ref, fb_col_ref, mnorm_ref,
                      y_ref, c_ref, n_ref, m_ref):
    c = pl.program_id(1)
    q = CHUNK

    @pl.when(c == 0)
    def _():
        c_ref[...] = jnp.zeros(c_ref.shape, F32)
        n_ref[...] = jnp.zeros(n_ref.shape, F32)
        m_ref[...] = jnp.zeros(m_ref.shape, F32)

    sm = sm_ref[...]
    sm_t = sm.T
    li_col = sm[:, S_IG:S_IG + M_HEADS] + ib_row_ref[...]
    lf_col = _log_sigmoid(sm[:, S_FG:S_FG + M_HEADS] + fb_row_ref[...])
    li_row = sm_t[S_IG:S_IG + M_HEADS, :] + ib_col_ref[...]
    lf_row = _log_sigmoid(sm_t[S_FG:S_FG + M_HEADS, :] + fb_col_ref[...])
    tril = _tri(q, True)
    b_col = _dot3_r(tril.astype(BF16), lf_col)
    b_row = _dot3_l(lf_row, _tri(q, False).astype(BF16))
    b_last = b_col[q - 1:q, :]

    for h in range(M_HEADS):
        ksl = slice(h * M_DQK, (h + 1) * M_DQK)
        vsl = slice(h * M_DV, (h + 1) * M_DV)
        m_prev = m_ref[h:h + 1, 0:1]
        bc = b_col[:, h:h + 1]
        log_d = jnp.where(tril, bc - b_row[h:h + 1, :] + li_row[h:h + 1, :], -jnp.inf)
        inter = bc + m_prev
        s = jnp.maximum(inter, jnp.max(log_d, axis=1, keepdims=True))
        w_inter = jnp.exp(inter - s)
        qh = q_ref[:, ksl]
        qh_bf = qh.astype(BF16)
        kh = k_ref[:, ksl] * (M_DQK ** -0.5)
        vh_bf = v_ref[:, vsl].astype(BF16)
        qk = _dot_rt(qh_bf, kh.astype(BF16)) * jnp.exp(log_d - s)
        c_prev = c_ref[h]
        n_prev = n_ref[h:h + 1, :]
        num = _dot(qk.astype(BF16), vh_bf) + w_inter * _dot(qh_bf, c_prev.astype(BF16))
        den = jnp.sum(qk, axis=1, keepdims=True) + w_inter * jnp.sum(qh * n_prev, axis=1, keepdims=True)
        hm = num / jnp.maximum(jnp.abs(den), jnp.exp(-s))
        hn = hm * lax.rsqrt(jnp.mean(hm * hm, axis=-1, keepdims=True) + EPS) * mnorm_ref[:, vsl]
        y_ref[:, vsl] = (hn * jax.nn.sigmoid(o_ref[:, vsl])).astype(y_ref.dtype)

        m_new = s[q - 1:q, :]
        wk = jnp.exp(b_last[:, h:h + 1] - bc + li_col[:, h:h + 1] - m_new)
        decay = jnp.exp(b_last[:, h:h + 1] + m_prev - m_new)
        kw = kh * wk
        c_ref[h] = decay * c_prev + _dot_lt(kw.astype(BF16), vh_bf)
        n_ref[h:h + 1, :] = decay * n_prev + jnp.sum(kw, axis=0, keepdims=True)
        m_ref[h:h + 1, :] = jnp.broadcast_to(m_new, (1, m_ref.shape[1]))


def _ml_prompt_specs(proj, n_chunks, n_batch, n_rows_total, p):
    q = CHUNK
    rb = lambda b, c: b * n_chunks + c

    def pspec(width, name):
        return pl.BlockSpec((q, width), lambda b, c: (rb(b, c), PROJ_SRC[name][1] // width))

    def wspec(shape):
        return pl.BlockSpec(shape, lambda b, c: (0,) * len(shape))

    return dict(
        in_specs=[pspec(1024, "q"), pspec(1024, "k"), pspec(2048, "v"), pspec(2048, "o"), pspec(SM_W, "s"),
                  wspec((1, 8)), wspec((8, 1)), wspec((1, 8)), wspec((8, 1)), wspec((1, 2048))],
        args=[*_proj_arrays(proj, "qkvos"),
              p["ib_row"], p["ib_col"], p["fb_row"], p["fb_col"], p["m_norm"]],
        out_specs=[pl.BlockSpec((q, D_MODEL), lambda b, c: (rb(b, c), 0)),
                   pl.BlockSpec((None, M_HEADS, M_DQK, M_DV), lambda b, c: (b, 0, 0, 0)),
                   pl.BlockSpec((None, M_HEADS, M_DQK), lambda b, c: (b, 0, 0)),
                   pl.BlockSpec((None, M_HEADS, 128), lambda b, c: (b, 0, 0))],
        out_shape=[jax.ShapeDtypeStruct((n_rows_total, D_MODEL), BF16),
                   jax.ShapeDtypeStruct((n_batch, M_HEADS, M_DQK, M_DV), F32),
                   jax.ShapeDtypeStruct((n_batch, M_HEADS, M_DQK), F32),
                   jax.ShapeDtypeStruct((n_batch, M_HEADS, 128), F32)])


def _ml_sample_kernel(q_ref, k_ref, v_ref, o_ref, sm_ref, ib_row_ref, fb_row_ref, mnorm_ref,
                      cin_ref, nin_ref, min_ref,
                      y_ref, cout_ref, nout_ref, mout_ref):
    r8 = SAMPLE_ROWS_PER_STEP
    li = sm_ref[:, S_IG:S_IG + M_HEADS] + ib_row_ref[...]
    lf = _log_sigmoid(sm_ref[:, S_FG:S_FG + M_HEADS] + fb_row_ref[...])
    m_prev = min_ref[...]
    inter = lf + m_prev
    s = jnp.maximum(inter, li)
    w_inter = jnp.exp(inter - s)
    w_in = jnp.exp(li - s)
    mout_ref[...] = s
    row_k = lax.broadcasted_iota(jnp.int32, (r8, M_DQK), 0)

    for h in range(M_HEADS):
        ksl = slice(h * M_DQK, (h + 1) * M_DQK)
        vsl = slice(h * M_DV, (h + 1) * M_DV)
        qh = q_ref[:, ksl]
        kh = k_ref[:, ksl] * (M_DQK ** -0.5)
        vh = v_ref[:, vsl]
        vh_bf = vh.astype(BF16)
        n_prev = nin_ref[:, ksl]
        wi = w_inter[:, h:h + 1]
        qk = jnp.sum(qh * kh, axis=1, keepdims=True) * w_in[:, h:h + 1]
        kw = kh * w_in[:, h:h + 1]
        qc = jnp.zeros((r8, M_DV), F32)
        for r in range(r8):
            c_prev = cin_ref[r, h]
            q_r = jnp.where(row_k == r, qh, 0.0).astype(BF16)
            qc = qc + _dot(q_r, c_prev.astype(BF16))
            kw_r = jnp.where(row_k == r, kw, 0.0).astype(BF16)
            cout_ref[r, h] = w_inter[r:r + 1, h:h + 1] * c_prev + _dot_lt(kw_r, vh_bf)
        nout_ref[:, ksl] = wi * n_prev + kw
        num = qk * vh + wi * qc
        den = qk + wi * jnp.sum(qh * n_prev, axis=1, keepdims=True)
        hm = num / jnp.maximum(jnp.abs(den), jnp.exp(-s[:, h:h + 1]))
        hn = hm * lax.rsqrt(jnp.mean(hm * hm, axis=-1, keepdims=True) + EPS) * mnorm_ref[:, vsl]
        y_ref[:, vsl] = hn * jax.nn.sigmoid(o_ref[:, vsl])


def _ml_sample(proj, row0, n_dec, layer, c_all, n_all, m_all, p, prev):
    r8 = SAMPLE_ROWS_PER_STEP
    rb0 = row0 // r8

    def pspec(width, name):
        return pl.BlockSpec((r8, width), lambda i: (rb0 + i, PROJ_SRC[name][1] // width))

    def wspec(shape):
        return pl.BlockSpec(shape, lambda i: (0,) * len(shape))

    cspec = pl.BlockSpec((None, r8, M_HEADS, M_DQK, M_DV), lambda i: (layer, i, 0, 0, 0))
    nspec = pl.BlockSpec((None, r8, M_HEADS * M_DQK), lambda i: (layer, i, 0))
    mspec = pl.BlockSpec((None, r8, M_HEADS), lambda i: (layer, i, 0))
    in_specs = [pspec(1024, "q"), pspec(1024, "k"), pspec(2048, "v"), pspec(2048, "o"), pspec(SM_W, "s"),
                wspec((1, 8)), wspec((1, 8)), wspec((1, 2048)), cspec, nspec, mspec]
    args = [*_proj_arrays(proj, "qkvos"), p["ib_row"], p["fb_row"], p["m_norm"], c_all, n_all, m_all]
    n_in = len(args)
    aliases = {}
    if prev is not None:
        in_specs += [pl.BlockSpec(memory_space=pl.ANY)] * 3
        args += list(prev)
        aliases = {n_in: 1, n_in + 1: 2, n_in + 2: 3}
    return pl.pallas_call(
        _drop_refs(_ml_sample_kernel, n_in, len(aliases)),
        grid=(n_dec // r8,),
        in_specs=in_specs,
        out_specs=[pl.BlockSpec((r8, D_MODEL), lambda i: (i, 0)), cspec, nspec, mspec],
        out_shape=[jax.ShapeDtypeStruct((n_dec, D_MODEL), F32),
                   jax.ShapeDtypeStruct(c_all.shape, F32),
                   jax.ShapeDtypeStruct(n_all.shape, F32),
                   jax.ShapeDtypeStruct(m_all.shape, F32)],
        input_output_aliases=aliases,
        compiler_params=_cp("parallel"),
        name="ml_sample",
    )(*args)


FFN_TN = 512
FFN_SUB = 256


def _ffn_gate_prompt_kernel(a_ref, wg_ref, wu_ref, cw_ref, cb_ref, act_ref, tail_ref, *gpads):
    seq = a_ref.shape[0]
    a = a_ref[...]
    for h, gpad in enumerate(gpads):
        sl = slice(h * FFN_SUB, (h + 1) * FFN_SUB)
        g = _dot(a, wg_ref[:, sl].astype(BF16))
        u = _dot(a, wu_ref[:, sl].astype(BF16))
        gpad[0:8, :] = jnp.zeros((8, FFN_SUB), F32)
        gpad[8:seq + 8, :] = g
        acc = gpad[6:seq + 6, :] * cw_ref[0:1, sl]
        acc = acc + gpad[7:seq + 7, :] * cw_ref[1:2, sl]
        acc = acc + g * cw_ref[2:3, sl]
        act_ref[:, sl] = (_silu(acc + cb_ref[:, sl]) * u).astype(act_ref.dtype)
        tail_ref[:, sl] = g[seq - 8:seq, :]


def _ffn_gate_prompt(h2, w_gate, w_up, layer, n_batch, seq, n_rows_total, p):
    tn = FFN_TN
    k = h2.shape[1]
    wspec = pl.BlockSpec((None, k, tn), lambda b, j: (layer, 0, j))
    return pl.pallas_call(
        _ffn_gate_prompt_kernel,
        grid=(n_batch, D_FF // tn),
        in_specs=[_lhs_spec(seq, k), wspec, wspec,
                  pl.BlockSpec((3, tn), lambda b, j: (0, j)),
                  pl.BlockSpec((1, tn), lambda b, j: (0, j))],
        out_specs=[pl.BlockSpec((seq, tn), lambda b, j: (b, j)),
                   pl.BlockSpec((None, 8, tn), lambda b, j: (b, 0, j))],
        out_shape=[jax.ShapeDtypeStruct((n_rows_total, D_FF), BF16),
                   jax.ShapeDtypeStruct((n_batch, 8, D_FF), F32)],
        scratch_shapes=[pltpu.VMEM((seq + 8, FFN_SUB), F32)] * (tn // FFN_SUB),
        compiler_params=_cp("parallel", "arbitrary"),
        name="ffn_gate_prompt",
    )(h2, w_gate, w_up, p["ffn_cw"], p["ffn_cb"])


def _ffn_gate_sample_kernel(a_ref, wg_ref, wu_ref, s0_ref, s1_ref, cw_ref, cb_ref, act_any, act_ref, g_ref):
    del act_any
    a = a_ref[...]
    g = _dot(a, wg_ref[...].astype(BF16))
    u = _dot(a, wu_ref[...].astype(BF16))
    acc = s0_ref[...] * cw_ref[0:1, :]
    acc = acc + s1_ref[...] * cw_ref[1:2, :]
    acc = acc + g * cw_ref[2:3, :]
    act_ref[...] = (_silu(acc + cb_ref[...]) * u).astype(act_ref.dtype)
    g_ref[...] = g


def _ffn_gate_sample(h2, w_gate, w_up, layer, row0, n_dec, conv_state_all, p, act):
    tn = FFN_TN
    k = h2.shape[1]
    rb = row0 // n_dec
    nct = D_FF // tn
    wspec = pl.BlockSpec((None, k, tn), lambda j: (layer, 0, j))
    return pl.pallas_call(
        _ffn_gate_sample_kernel,
        grid=(nct,),
        in_specs=[pl.BlockSpec((n_dec, k), lambda j: (rb, 0)), wspec, wspec,
                  pl.BlockSpec((None, n_dec, tn), lambda j: (layer, 0, j)),
                  pl.BlockSpec((None, n_dec, tn), lambda j: (layer, 0, nct + j)),
                  pl.BlockSpec((3, tn), lambda j: (0, j)),
                  pl.BlockSpec((1, tn), lambda j: (0, j)),
                  pl.BlockSpec(memory_space=pl.ANY)],
        out_specs=[pl.BlockSpec((n_dec, tn), lambda j: (rb, j)),
                   pl.BlockSpec((n_dec, tn), lambda j: (0, j))],
        out_shape=[jax.ShapeDtypeStruct(act.shape, act.dtype),
                   jax.ShapeDtypeStruct((n_dec, D_FF), F32)],
        input_output_aliases={7: 0},
        compiler_params=_cp("arbitrary"),
        name="ffn_gate_sample",
    )(h2, w_gate, w_up, conv_state_all, conv_state_all, p["ffn_cw"], p["ffn_cb"], act)


def _pick_tile(m, candidates):
    for t in candidates:
        if m % t == 0:
            return t
    raise ValueError(f"no row tile for {m}")


def _layer_params(l, ssd_conv_w, ssd_conv_b, ssd_dt_bias, ssd_a_log, ssd_d, ssd_norm, gm_norm, gm_ws, gm_bs,
                  m_i_bias, m_f_bias, m_norm, ffn_conv_w, ffn_conv_b):
    cw, cb = ssd_conv_w[l], ssd_conv_b[l]
    return dict(
        cwx=cw[:, 0:2048], cbx=cb[None, 0:2048], cwb=cw[:, 2048:2560], cbb=cb[None, 2048:2560],
        cwc=cw[:, 2560:3072], cbc=cb[None, 2560:3072],
        dtb_row=ssd_dt_bias[l][None, :], dtb_col=ssd_dt_bias[l][:, None],
        alog_row=ssd_a_log[l][None, :], alog_col=ssd_a_log[l][:, None],
        dfull=jnp.repeat(ssd_d[l], SSD_HEAD_DIM)[None, :], ssd_norm=ssd_norm[l][None, :],
        gm_norm=gm_norm[l][None, :], gm_ws=gm_ws[l], gm_bst=gm_bs[l].T,
        gm_w0=jnp.repeat(gm_ws[l][:, 0, 0], GM_GROUP_DIM)[None, :],
        gm_b0=jnp.repeat(gm_bs[l][:, 0], GM_GROUP_DIM)[None, :],
        ib_row=m_i_bias[l][None, :], ib_col=m_i_bias[l][:, None],
        fb_row=m_f_bias[l][None, :], fb_col=m_f_bias[l][:, None], m_norm=m_norm[l][None, :],
        ffn_cw=ffn_conv_w[l], ffn_cb=ffn_conv_b[l][None, :],
    )


def kernel(x_prompt, x_sample, state_ssd, state_ssd_conv, state_mlstm_c, state_mlstm_n, state_mlstm_m,
           state_ffn_conv, norm1, w_in, ssd_conv_w, ssd_conv_b, ssd_dt_bias, ssd_a_log, ssd_d, ssd_norm,
           gm_norm, gm_ws, gm_bs, m_i_bias, m_f_bias, m_norm, w_branch, w_out, norm2,
           w_gate, w_up, ffn_conv_w, ffn_conv_b, w_down, final_norm):
    n_batch, seq, d = x_prompt.shape
    n_dec = x_sample.shape[0]
    depth = w_in.shape[0]
    n_chunks = seq // CHUNK
    n_prompt = n_batch * seq
    n_rows = n_prompt + n_dec
    tm = _pick_tile(n_rows, (832, 640, 384, 128))
    tm_wide = _pick_tile(n_rows, (1664, 832, 640, 384, 128))
    tm_big = _pick_tile(n_rows, (4160, 2080, 1664, 832, 640, 384, 128))
    tm_half = _pick_tile(n_rows, (2080, 1664, 832, 640, 384, 128))
    w_in_t = jnp.swapaxes(w_in, 1, 2)

    ssd_conv_all = state_ssd_conv.reshape(depth, n_dec, -1)
    ssd_h_all = state_ssd.reshape(depth, n_dec, D_MODEL, SSD_STATE)
    ml_n_all = state_mlstm_n.reshape(depth, n_dec, -1)
    ffn_conv_all = state_ffn_conv.reshape(depth, n_dec, -1)
    s_ssd = None
    s_ml = None

    def xbc(rows):
        c0 = PROJ_SRC["x"][1]
        return rows[..., c0:c0 + 3072]

    def tail_rows(a, n_tail):
        return jnp.stack([a[(b + 1) * seq - n_tail:(b + 1) * seq] for b in range(n_batch)])

    x = jnp.concatenate([x_prompt.reshape(n_prompt, d), x_sample.reshape(n_dec, d)], axis=0)
    outs = [[] for _ in range(13)]
    for l in range(depth):
        p = _layer_params(l, ssd_conv_w, ssd_conv_b, ssd_dt_bias, ssd_a_log, ssd_d, ssd_norm, gm_norm, gm_ws, gm_bs,
                          m_i_bias, m_f_bias, m_norm, ffn_conv_w, ffn_conv_b)
        h = _rmsnorm_rows(x, norm1[l], BF16, tm)
        proj = {"p": _matmul_t(h, w_in_t, l, tm_big, 256, (W_IN_B, W_IN_G, W_IN_A)),
                "s": _matmul_small(h, w_in_t, l, tm_half)}

        y0, p_ssd, y2, p_c, p_n, p_m = _prompt_scans(proj, n_batch, n_chunks, n_rows, p)
        y1 = _gm_prompt(proj, n_prompt, n_rows, p)

        ys0, s_ssd = _ssd_sample(proj, n_prompt, n_dec, l, ssd_conv_all, ssd_h_all, p, s_ssd)
        ys2, *s_ml = _ml_sample(proj, n_prompt, n_dec, l, state_mlstm_c, ml_n_all, state_mlstm_m, p, s_ml)
        y0, y1, y2, s_vn = _gm_sample_pack(proj, n_prompt, n_dec, p, ys0, ys2, y0, y1, y2)

        merged = _merge(y0, y1, y2, w_branch, l, proj, tm_wide, 256)
        x = _matmul(merged, w_out, l, tm_big, 256, residual=x)

        h2 = _rmsnorm_rows(x, norm2[l], BF16, tm)
        act, g_tail = _ffn_gate_prompt(h2, w_gate, w_up, l, n_batch, seq, n_rows, p)
        act, g_dec = _ffn_gate_sample(h2, w_gate, w_up, l, n_prompt, n_dec, ffn_conv_all, p, act)
        x = _matmul(act, w_down, l, tm_wide, 256, residual=x)

        outs[0].append(p_ssd.reshape(n_batch, SSD_HEADS, SSD_HEAD_DIM, SSD_STATE))
        outs[1].append(xbc(tail_rows(proj["p"], 3)))
        outs[2].append(p_c)
        outs[3].append(p_n)
        outs[4].append(p_m[:, :, 0])
        outs[5].append(g_tail[:, 6:8, :])
        outs[7].append(jnp.concatenate([state_ssd_conv[l][:, 1:], xbc(proj["p"][n_prompt:])[:, None, :]], axis=1))
        outs[11].append(jnp.concatenate([state_ffn_conv[l][:, 1:], g_dec[:, None, :]], axis=1))
        outs[12].append(s_vn.reshape(n_dec, 1, D_MODEL))

    tf = _pick_tile(n_prompt, (1024, 512, 256, 128))
    y_prompt = _rmsnorm_rows(x, final_norm, F32, tf, 0, n_prompt).reshape(n_batch, seq, d)
    y_sample = _rmsnorm_rows(x, final_norm, F32, n_dec, n_prompt // n_dec, n_dec).reshape(n_dec, 1, d)
    s_c, s_n, s_m = s_ml
    stacked = {6: s_ssd.reshape(depth, n_dec, SSD_HEADS, SSD_HEAD_DIM, SSD_STATE), 8: s_c,
               9: s_n.reshape(depth, n_dec, M_HEADS, M_DQK), 10: s_m}
    return (y_prompt, y_sample) + tuple(stacked[i] if i in stacked else jnp.stack(outs[i]) for i in range(13))
```
